```python
import jax, jax.numpy as jnp
from jax import lax
import numpy as np

D_MODEL = 1024
BATCH = 8
SEQ = 8192
DEPTH = 1

N_META = 16
D_MIX = D_MODEL
LRU_WIDTH = D_MIX // 2
LRU_BLOCKS = 8
LRU_BLOCK = LRU_WIDTH // LRU_BLOCKS
LRU_C = 8.0
CONV_WIDTH = 4
CONV_LEFT = 2
RET_WIDTH = D_MIX - LRU_WIDTH
RET_HEADS = 4
RET_HEAD_DIM = RET_WIDTH // RET_HEADS
CHUNK = 128
ROPE_BASE = 10000.0
N_EXPERTS = 16
EXPERT_FF = 1024
EC_CAPACITY = 2
EPS = 1e-6
IN_COLS = 2 * LRU_WIDTH + 4 * RET_WIDTH

kernel_name = 'hymba_lru_retention_ec_moe_encoder'


def _rmsnorm(x, g):
    xf = x.astype(jnp.float32)
    y = xf * lax.rsqrt(jnp.mean(xf * xf, axis=-1, keepdims=True) + EPS)
    return (y * g.astype(jnp.float32)).astype(x.dtype)


def _centred_depthwise_conv(u, w, b):
    y = lax.conv_general_dilated(
        u, w[:, None, :].astype(u.dtype), window_strides=(1,),
        padding=[(CONV_LEFT, CONV_WIDTH - 1 - CONV_LEFT)],
        dimension_numbers=('NWC', 'WIO', 'NWC'),
        feature_group_count=u.shape[-1])
    return y + b.astype(u.dtype)


def _block_diag(u, w, b):
    bsz, t, _ = u.shape
    ub = u.reshape(bsz, t, LRU_BLOCKS, LRU_BLOCK)
    return jnp.einsum('btnc,ncd->btnd', ub, w).reshape(bsz, t, LRU_WIDTH) + b


def _linear_scan_op(e1, e2):
    a1, b1 = e1
    a2, b2 = e2
    return a1 * a2, a2 * b1 + b2


def _rg_lru(u, w_r, b_r, w_i, b_i, lam, reverse):
    r = jax.nn.sigmoid(_block_diag(u, w_r, b_r))
    i = jax.nn.sigmoid(_block_diag(u, w_i, b_i))
    log_a = -LRU_C * r * jax.nn.softplus(-lam)
    a = jnp.exp(log_a)
    gated_x = jnp.sqrt(-jnp.expm1(2.0 * log_a)) * (i * u)
    _, h = lax.associative_scan(_linear_scan_op, (a, gated_x), axis=1, reverse=reverse)
    return h


def _recurrent_group(xb, gate, conv_w, conv_b, w_r, b_r, w_i, b_i, lam, out_g):
    f32 = jnp.float32
    u = _centred_depthwise_conv(xb, conv_w, conv_b).astype(f32)
    h_fwd = _rg_lru(u, w_r[0].astype(f32), b_r[0].astype(f32), w_i[0].astype(f32),
                    b_i[0].astype(f32), lam[0].astype(f32), False)
    h_bwd = _rg_lru(u, w_r[1].astype(f32), b_r[1].astype(f32), w_i[1].astype(f32),
                    b_i[1].astype(f32), lam[1].astype(f32), True)
    y = (h_fwd + h_bwd) * jax.nn.gelu(gate.astype(f32))
    return _rmsnorm(y, out_g).astype(xb.dtype)


def _rotary(t, pos):
    half = t.shape[-1] // 2
    freqs = ROPE_BASE ** (-jnp.arange(half, dtype=jnp.float32) / half)
    ang = pos.astype(jnp.float32)[:, None] * freqs[None, :]
    cos = jnp.cos(ang)[None, :, None, :]
    sin = jnp.sin(ang)[None, :, None, :]
    t1, t2 = t[..., :half], t[..., half:]
    return jnp.concatenate([t1 * cos - t2 * sin, t1 * sin + t2 * cos], axis=-1)


def _to_chunks(t, pad):
    bsz, _, h, d = t.shape
    t = jnp.pad(t, ((0, 0), (pad, 0), (0, 0), (0, 0)))
    n = t.shape[1] // CHUNK
    return t.reshape(bsz, n, CHUNK, h, d).transpose(0, 3, 1, 2, 4)


def _exclusive_chunk_states(u, chunk_decay, reverse):
    un = jnp.moveaxis(u, 2, 0)

    def step(carry, ui):
        return carry * chunk_decay[None, :, None, None] + ui, carry

    _, excl = lax.scan(step, jnp.zeros_like(un[0]), un, reverse=reverse)
    return excl


def _retention_group(q, k, v, g, out_g):
    bsz, t, _ = q.shape
    f32 = jnp.float32
    pos = jnp.arange(t)

    def heads(z):
        return z.astype(f32).reshape(bsz, t, RET_HEADS, RET_HEAD_DIM)

    qh = _rotary(heads(q), pos) * (RET_HEAD_DIM ** -0.5)
    kh = _rotary(heads(k), pos)
    vh = heads(v)
    pad = (-t) % CHUNK
    qc, kc, vc = _to_chunks(qh, pad), _to_chunks(kh, pad), _to_chunks(vh, pad)

    log_g = jnp.log(1.0 - jnp.exp2(-5.0 - jnp.arange(RET_HEADS, dtype=f32)))
    idx = jnp.arange(CHUNK, dtype=f32)
    dist = jnp.abs(idx[:, None] - idx[None, :])
    intra_decay = jnp.exp(log_g[:, None, None] * dist)
    scores = jnp.einsum('bhncd,bhnmd->bhncm', qc, kc) * intra_decay[None, :, None]
    o = jnp.einsum('bhncm,bhnme->bhnce', scores, vc)

    def pos_decay(expo):
        return jnp.exp(log_g[:, None] * expo[None, :])[None, :, None, :, None]

    chunk_decay = jnp.exp(log_g * CHUNK)
    u_fwd = jnp.einsum('bhncd,bhnce->bhnde', kc * pos_decay(CHUNK - 1.0 - idx), vc)
    u_bwd = jnp.einsum('bhncd,bhnce->bhnde', kc * pos_decay(idx), vc)
    state_fwd = _exclusive_chunk_states(u_fwd, chunk_decay, False)
    state_bwd = _exclusive_chunk_states(u_bwd, chunk_decay, True)
    o = (o
         + jnp.einsum('bhncd,nbhde->bhnce', qc * pos_decay(idx + 1.0), state_fwd)
         + jnp.einsum('bhncd,nbhde->bhnce', qc * pos_decay(CHUNK - idx), state_bwd))

    o = o.transpose(0, 2, 3, 1, 4).reshape(bsz, -1, RET_HEADS, RET_HEAD_DIM)[:, pad:]
    o = o * lax.rsqrt(jnp.mean(o * o, axis=-1, keepdims=True) + EPS)
    o = o.reshape(bsz, t, RET_WIDTH) * out_g.astype(f32)
    return (o * jax.nn.silu(g.astype(f32))).astype(q.dtype)


def _mixer(h, w_in, conv_w, conv_b, lru_w_r, lru_b_r, lru_w_i, lru_b_i, lru_lambda,
           lru_out_g, ret_out_g, w_out):
    z = jnp.einsum('btd,dc->btc', h, w_in)
    o1 = LRU_WIDTH
    o2 = 2 * LRU_WIDTH
    lru_x, lru_gate = z[..., :o1], z[..., o1:o2]
    q = z[..., o2:o2 + RET_WIDTH]
    k = z[..., o2 + RET_WIDTH:o2 + 2 * RET_WIDTH]
    v = z[..., o2 + 2 * RET_WIDTH:o2 + 3 * RET_WIDTH]
    g = z[..., o2 + 3 * RET_WIDTH:]
    y_lru = _recurrent_group(lru_x, lru_gate, conv_w, conv_b, lru_w_r, lru_b_r,
                             lru_w_i, lru_b_i, lru_lambda, lru_out_g)
    y_ret = _retention_group(q, k, v, g, ret_out_g)
    y = jnp.concatenate([y_lru, y_ret], axis=-1)
    return jnp.einsum('btc,cd->btd', y, w_out)


def _expert_choice_ffn(h, w_router, w_gate, w_up, w_down):
    bsz, t, d = h.shape
    cap = EC_CAPACITY * t // N_EXPERTS
    aff = jax.nn.softmax(jnp.einsum('btd,de->bte', h, w_router).astype(jnp.float32), axis=-1)
    gate, idx = lax.top_k(aff.transpose(0, 2, 1), cap)
    xs = jax.vmap(lambda hb, ib: hb[ib])(h, idx)
    a = jnp.einsum('becd,edf->becf', xs, w_gate)
    u = jnp.einsum('becd,edf->becf', xs, w_up)
    y = jnp.einsum('becf,efd->becd', jax.nn.silu(a) * u, w_down)
    y = y * gate[..., None].astype(y.dtype)
    flat = (jnp.arange(bsz)[:, None, None] * t + idx).reshape(-1)
    out = jax.ops.segment_sum(y.reshape(-1, d), flat, num_segments=bsz * t)
    return out.reshape(bsz, t, d)


def setup_inputs(seed: int = 0) -> dict:
    key = jax.random.key(seed)
    ks = jax.random.split(key, 20)
    f32 = jnp.float32
    L = DEPTH

    def nrm(k, shape, scale):
        return jax.random.normal(k, shape, f32) * scale

    u = jax.random.uniform(ks[9], (L, 2, LRU_WIDTH), f32, 0.9, 0.999)
    s = u ** (1.0 / LRU_C)
    lam = jnp.log(s) - jnp.log1p(-s)
    return {
        'x': nrm(ks[0], (BATCH, SEQ, D_MODEL), 1.0),
        'meta_tokens': nrm(ks[1], (N_META, D_MODEL), 1.0),
        'norm1_g': 1.0 + nrm(ks[2], (L, D_MODEL), 0.02),
        'w_in': nrm(ks[3], (L, D_MODEL, IN_COLS), D_MODEL ** -0.5),
        'conv_w': nrm(ks[4], (L, CONV_WIDTH, LRU_WIDTH), CONV_WIDTH ** -0.5),
        'conv_b': nrm(ks[5], (L, LRU_WIDTH), 0.02),
        'lru_w_r': nrm(ks[6], (L, 2, LRU_BLOCKS, LRU_BLOCK, LRU_BLOCK), LRU_BLOCK ** -0.5),
        'lru_b_r': nrm(ks[7], (L, 2, LRU_WIDTH), 0.02),
        'lru_w_i': nrm(ks[8], (L, 2, LRU_BLOCKS, LRU_BLOCK, LRU_BLOCK), LRU_BLOCK ** -0.5),
        'lru_b_i': nrm(ks[10], (L, 2, LRU_WIDTH), 0.02),
        'lru_lambda': lam,
        'lru_out_g': 1.0 + nrm(ks[11], (L, LRU_WIDTH), 0.02),
        'ret_out_g': 1.0 + nrm(ks[12], (L, RET_WIDTH), 0.02),
        'w_out': nrm(ks[13], (L, D_MIX, D_MODEL), D_MIX ** -0.5),
        'norm2_g': 1.0 + nrm(ks[14], (L, D_MODEL), 0.02),
        'w_router': nrm(ks[15], (L, D_MODEL, N_EXPERTS), D_MODEL ** -0.5),
        'w_gate': nrm(ks[16], (L, N_EXPERTS, D_MODEL, EXPERT_FF), D_MODEL ** -0.5),
        'w_up': nrm(ks[17], (L, N_EXPERTS, D_MODEL, EXPERT_FF), D_MODEL ** -0.5),
        'w_down': nrm(ks[18], (L, N_EXPERTS, EXPERT_FF, D_MODEL), EXPERT_FF ** -0.5),
        'final_g': 1.0 + nrm(ks[19], (D_MODEL,), 0.02),
    }


def reference(x, meta_tokens, norm1_g, w_in, conv_w, conv_b, lru_w_r, lru_b_r, lru_w_i,
              lru_b_i, lru_lambda, lru_out_g, ret_out_g, w_out, norm2_g, w_router, w_gate,
              w_up, w_down, final_g):
    bsz = x.shape[0]
    meta = jnp.broadcast_to(meta_tokens.astype(x.dtype)[None], (bsz, N_META, x.shape[-1]))
    hs = jnp.concatenate([meta, x], axis=1)
    for l in range(DEPTH):
        h = _rmsnorm(hs, norm1_g[l])
        hs = hs + _mixer(h, w_in[l], conv_w[l], conv_b[l], lru_w_r[l], lru_b_r[l], lru_w_i[l],
                         lru_b_i[l], lru_lambda[l], lru_out_g[l], ret_out_g[l], w_out[l])
        h = _rmsnorm(hs, norm2_g[l])
        hs = hs + _expert_choice_ffn(h, w_router[l], w_gate[l], w_up[l], w_down[l])
    return _rmsnorm(hs[:, N_META:], final_g)
```

```python
import functools

import jax
import jax.numpy as jnp
from jax import lax
from jax.experimental import pallas as pl
from jax.experimental.pallas import tpu as pltpu

N_META = 16
LRU_BLOCKS = 8
LRU_C = 8.0
CONV_WIDTH = 4
CONV_LEFT = 2
RET_HEADS = 4
CHUNK = 128
ROPE_BASE = 10000.0
N_EXPERTS = 16
EC_CAPACITY = 2
EPS = 1e-6

LANES = 128
SUBLANES = 8
BF16_ROWS = 16
MXU_DIM = 256

DISPATCH_WINDOW = 128
COMBINE_WINDOW = 3 * BF16_ROWS

F32 = jnp.float32
BF16 = jnp.bfloat16


def _round_up(x, m):
    return (x + m - 1) // m * m


def _largest_divisor_leq(n, k):
    for d in range(min(n, k), 0, -1):
        if n % d == 0:
            return d
    return 1


def _inproj_kernel(hs_ref, g_ref, w_ref, cos_ref, sin_ref, z_ref, *, cb, hd, q_scale):
    x = hs_ref[...]
    ms = jnp.mean(x * x, axis=-1, keepdims=True)
    h = (x * lax.rsqrt(ms + EPS) * g_ref[...]).astype(BF16)
    cos = cos_ref[...]
    sin = sin_ref[...]
    for j in range(w_ref.shape[1] // cb):
        zj = jnp.dot(h, w_ref[:, j * cb:(j + 1) * cb], preferred_element_type=F32)
        if j in (2, 3):
            for hh in range(cb // hd):
                t = zj[:, hh * hd:(hh + 1) * hd]
                r = pltpu.roll(t, hd // 2, axis=1)
                t = t * cos + r * sin
                if j == 2:
                    t = t * q_scale
                z_ref[:, j * cb + hh * hd:j * cb + (hh + 1) * hd] = t
        else:
            z_ref[:, j * cb:(j + 1) * cb] = zj


def _scan_tile(a_ref, b_ref, dst_ref, dst_row0, carry_ref, n_rows, forward):
    ng = n_rows // SUBLANES
    row = lax.broadcasted_iota(jnp.int32, (SUBLANES, a_ref.shape[1]), 0)

    def body(g, carry):
        gg = g if forward else ng - 1 - g
        r0 = pl.multiple_of(gg * SUBLANES, SUBLANES)
        a = a_ref[pl.ds(r0, SUBLANES), :]
        b = b_ref[pl.ds(r0, SUBLANES), :]
        for d in (1, 2, 4):
            sh = d if forward else SUBLANES - d
            a_s = pltpu.roll(a, sh, axis=0)
            b_s = pltpu.roll(b, sh, axis=0)
            valid = (row >= d) if forward else (row < SUBLANES - d)
            b = jnp.where(valid, a * b_s + b, b)
            a = jnp.where(valid, a * a_s, a)
        h = a * carry + b
        dst_ref[pl.ds(dst_row0 + r0, SUBLANES), :] = h
        return h[SUBLANES - 1:SUBLANES, :] if forward else h[0:1, :]

    carry_ref[...] = lax.fori_loop(0, ng, body, carry_ref[...])


def _lru_kernel(xm_ref, xp_ref, xn_ref, gate_ref, cw_ref, cb_ref, wr_ref, wi_ref, br_ref, bi_ref,
                lam_ref, og_ref, y_ref, hb_ref, a_ref, b_ref, xpad_ref, carry_ref, *, n_t, t_t, pad):
    p = pl.program_id(1)
    j = pl.program_id(2)
    ti = jnp.where(p == 0, n_t - 1 - j, j)
    w = xm_ref.shape[1]

    @pl.when(j == 0)
    def _():
        carry_ref[...] = jnp.zeros_like(carry_ref)

    xpad_ref[0:SUBLANES, :] = jnp.where(ti == 0, 0.0, xp_ref[...])
    xpad_ref[SUBLANES:SUBLANES + t_t, :] = xm_ref[...]
    xpad_ref[SUBLANES + t_t:, :] = jnp.where(ti == n_t - 1, 0.0, xn_ref[...])
    u = jnp.zeros((t_t, w), F32) + cb_ref[...]
    for k in range(CONV_WIDTH):
        u = u + cw_ref[k:k + 1, :] * xpad_ref[pl.ds(SUBLANES - CONV_LEFT + k, t_t), :]

    ub = u.astype(BF16)
    nh = w // MXU_DIM
    pre_r = jnp.concatenate(
        [jnp.dot(ub[:, q * MXU_DIM:(q + 1) * MXU_DIM], wr_ref[q], preferred_element_type=F32)
         for q in range(nh)], axis=1) + br_ref[...]
    pre_i = jnp.concatenate(
        [jnp.dot(ub[:, q * MXU_DIM:(q + 1) * MXU_DIM], wi_ref[q], preferred_element_type=F32)
         for q in range(nh)], axis=1) + bi_ref[...]
    r = jax.nn.sigmoid(pre_r)
    i = jax.nn.sigmoid(pre_i)
    nl = -lam_ref[...]
    en = jnp.exp(-jnp.abs(nl))
    w1 = 1.0 + en
    log1p_en = jnp.where(w1 == 1.0, en, en * jnp.log(w1) / jnp.where(w1 == 1.0, 1.0, w1 - 1.0))
    softplus = jnp.maximum(nl, 0.0) + log1p_en
    log_a = (-LRU_C) * r * softplus
    a = jnp.exp(log_a)
    gx = jnp.sqrt(-jnp.tanh(log_a) * (1.0 + a * a)) * (i * u)
    rowidx = ti * t_t + lax.broadcasted_iota(jnp.int32, (t_t, 1), 0)
    gx = jnp.where(jnp.logical_or(p == 0, rowidx >= pad), gx, 0.0)
    a_ref[...] = a
    b_ref[...] = gx

    @pl.when(p == 0)
    def _():
        _scan_tile(a_ref, b_ref, hb_ref, pl.multiple_of(ti * t_t, SUBLANES), carry_ref, t_t, False)

    @pl.when(p == 1)
    def _():
        _scan_tile(a_ref, b_ref, b_ref, 0, carry_ref, t_t, True)
        h = b_ref[...] + hb_ref[pl.ds(pl.multiple_of(ti * t_t, SUBLANES), t_t), :]
        y = h * jax.nn.gelu(gate_ref[...], approximate=True)
        ms = jnp.mean(y * y, axis=-1, keepdims=True)
        y_ref[...] = y * lax.rsqrt(ms + EPS) * og_ref[...]


def _retention_kernel(q_ref, k_ref, v_ref, g_ref, dmat_ref, dvec_ref, cd_ref, og_ref, y_ref,
                      sb_ref, s_ref, *, n_t, tc, hd):
    p = pl.program_id(1)
    j = pl.program_id(2)
    ti = jnp.where(p == 0, n_t - 1 - j, j)
    nheads = q_ref.shape[1] // hd

    @pl.when(j == 0)
    def _():
        s_ref[...] = jnp.zeros_like(s_ref)

    def tn_dot(x, y):
        return lax.dot_general(x, y, (((0,), (0,)), ((), ())), preferred_element_type=F32)

    def nt_dot(x, y):
        return lax.dot_general(x, y, (((1,), (1,)), ((), ())), preferred_element_type=F32)

    @pl.when(p == 0)
    def _():
        for c in reversed(range(tc)):
            cg = ti * tc + c
            for h in range(nheads):
                rs = slice(c * CHUNK, (c + 1) * CHUNK)
                cs = slice(h * hd, (h + 1) * hd)
                kk = k_ref[rs, cs]
                vv = v_ref[rs, cs].astype(BF16)
                sb_ref[cg, h] = s_ref[h].astype(BF16)
                s_ref[h] = s_ref[h] * cd_ref[h] + tn_dot((kk * dvec_ref[3, h]).astype(BF16), vv)

    @pl.when(p == 1)
    def _():
        for c in range(tc):
            cg = ti * tc + c
            for h in range(nheads):
                rs = slice(c * CHUNK, (c + 1) * CHUNK)
                cs = slice(h * hd, (h + 1) * hd)
                qq = q_ref[rs, cs]
                kk = k_ref[rs, cs]
                vv = v_ref[rs, cs].astype(BF16)
                sc = nt_dot(qq.astype(BF16), kk.astype(BF16)) * dmat_ref[h]
                o = jnp.dot(sc.astype(BF16), vv, preferred_element_type=F32)
                o = o + jnp.dot((qq * dvec_ref[0, h]).astype(BF16), s_ref[h].astype(BF16),
                                preferred_element_type=F32)
                o = o + jnp.dot((qq * dvec_ref[1, h]).astype(BF16), sb_ref[cg, h],
                                preferred_element_type=F32)
                s_ref[h] = s_ref[h] * cd_ref[h] + tn_dot((kk * dvec_ref[2, h]).astype(BF16), vv)
                ms = jnp.mean(o * o, axis=-1, keepdims=True)
                o = o * lax.rsqrt(ms + EPS) * og_ref[:, cs]
                y_ref[rs, cs] = o * jax.nn.silu(g_ref[rs, cs])


def _split3(x):
    hi = x.astype(BF16)
    r1 = x - hi.astype(F32)
    mid = r1.astype(BF16)
    lo = (r1 - mid.astype(F32)).astype(BF16)
    return hi, mid, lo


def _outproj_kernel(yl_ref, yr_ref, hs_ref, w_ref, g_ref, wrh_ref, wrm_ref, hs2_ref, h2e_ref, lg_ref, *, ne):
    wl = yl_ref.shape[1]
    d = hs_ref.shape[1]
    mix = jnp.dot(yl_ref[...].astype(BF16), w_ref[0:wl, :], preferred_element_type=F32)
    mix = mix + jnp.dot(yr_ref[...].astype(BF16), w_ref[wl:, :], preferred_element_type=F32)
    hs2 = hs_ref[...] + mix
    hs2_ref[...] = hs2
    ms = jnp.mean(hs2 * hs2, axis=-1, keepdims=True)
    h2 = hs2 * lax.rsqrt(ms + EPS) * g_ref[...]
    hh = h2.astype(BF16)
    h2e_ref[:, 0:d] = hh
    hm = (h2 - hh.astype(F32)).astype(BF16)
    ph = wrh_ref[...]
    lr = (jnp.dot(hh, ph, preferred_element_type=F32)
          + (jnp.dot(hh, wrm_ref[...], preferred_element_type=F32)
             + jnp.dot(hm, ph, preferred_element_type=F32)))
    lg_ref[...] = lr.T[0:ne, :]

    lane = lax.broadcasted_iota(jnp.int32, lr.shape, 1)
    live = lane < ne
    mx = jnp.max(jnp.where(live, lr, -jnp.inf), axis=1, keepdims=True)
    ex = jnp.where(live, jnp.exp(lr - mx), 0.0)
    aff = ex / jnp.sum(ex, axis=1, keepdims=True)
    a_hi = aff.astype(BF16).astype(F32)
    r1 = aff - a_hi
    a_mid = r1.astype(BF16).astype(F32)
    a_lo = (r1 - a_mid).astype(BF16).astype(F32)
    ext = a_hi + pltpu.roll(a_mid, ne, axis=1) + pltpu.roll(a_lo, 2 * ne, axis=1)
    h2e_ref[:, d:] = ext.astype(BF16)


def _route_kernel(lg_ref, slot_ref, off_ref, *, pad, tp, cap):
    ne, nc, _ = lg_ref.shape
    lg = lg_ref[...]
    m = jnp.max(lg, axis=0, keepdims=True)
    ex = jnp.exp(lg - m)
    aff = ex / jnp.sum(ex, axis=0, keepdims=True)
    tpos = (lax.broadcasted_iota(jnp.int32, (nc, LANES), 0) * LANES
            + lax.broadcasted_iota(jnp.int32, (nc, LANES), 1))
    valid = jnp.logical_and(tpos >= pad, tpos < tp)[None]
    affm = jnp.where(valid, aff, -1.0)

    def count(mask):
        c = jnp.sum(jnp.where(mask, 1.0, 0.0), axis=2, keepdims=True)
        return jnp.sum(c, axis=1, keepdims=True)

    def as_float(bits):
        return lax.bitcast_convert_type(bits, F32)

    def bs_body(_, lohi):
        lo, hi = lohi
        mid = lo + lax.shift_right_logical(hi - lo, 1)
        ok = count(affm >= as_float(mid)) >= float(cap)
        return jnp.where(ok, mid, lo), jnp.where(ok, hi, mid)

    lo0 = jnp.zeros((ne, 1, 1), jnp.int32)
    hi0 = jnp.full((ne, 1, 1), 0x3F800001, jnp.int32)
    thr_bits, _ = lax.fori_loop(0, 31, bs_body, (lo0, hi0))
    thr = as_float(thr_bits)

    gt = affm > thr
    eq = affm == thr
    need = float(cap) - count(gt)

    ci = lax.broadcasted_iota(jnp.int32, (LANES, LANES), 0)
    cj = lax.broadcasted_iota(jnp.int32, (LANES, LANES), 1)
    tri = (ci <= cj).astype(BF16)
    ri = lax.broadcasted_iota(jnp.int32, (nc, nc), 0)
    rj = lax.broadcasted_iota(jnp.int32, (nc, nc), 1)
    ltri = (rj < ri).astype(BF16)

    def prefix(mask2d):
        within = jnp.dot(mask2d.astype(BF16), tri, preferred_element_type=F32)
        tot = jnp.broadcast_to(within[:, LANES - 1:LANES], (nc, LANES)).astype(BF16)
        off = jnp.dot(ltri, tot, preferred_element_type=F32)
        return within + off, off

    for e in range(ne):
        tie_incl, _ = prefix(eq[e])
        tie_excl = tie_incl - eq[e].astype(F32)
        sel = jnp.logical_or(gt[e], jnp.logical_and(eq[e], tie_excl < need[e]))
        cum, off = prefix(sel)
        slot_ref[e] = jnp.where(sel, cum.astype(jnp.int32) - 1, -1)
        off_ref[e] = off.astype(jnp.int32)


def _dispatch_kernel(start_ref, h2e_ref, slot_ref, xs_hbm, xs_ref, sem, *, tcd, rw, s_rows, nc1, ne):
    b = pl.program_id(0)
    g = pl.program_id(1)
    i = pl.program_id(2)
    ge = slot_ref.shape[0]
    dext = h2e_ref.shape[1]

    @pl.when(i == 0)
    def _():
        xs_ref[...] = jnp.zeros_like(xs_ref)

    base = (b * nc1 + i * tcd) * ne + g * ge
    s0 = [start_ref[base + e] for e in range(ge)]
    s1 = [start_ref[base + tcd * ne + e] for e in range(ge)]
    lo = [(s0[e] // BF16_ROWS) * BF16_ROWS for e in range(ge)]
    npass = functools.reduce(jnp.maximum, [(s1[e] - lo[e] + rw - 1) // rw for e in range(ge)])
    r_iota = lax.broadcasted_iota(jnp.int32, (rw, 1), 0)

    def pass_body(q, _):
        ws = [pl.multiple_of(jnp.minimum(lo[e] + q * rw, s_rows - rw), BF16_ROWS) for e in range(ge)]
        hits = []
        for e in range(ge):
            slot_row = slot_ref[e:e + 1, :]
            hit = jnp.logical_and(slot_row == (ws[e] + r_iota), slot_row >= lo[e] + q * rw)
            hits.append(hit.astype(BF16))
        pc = jnp.concatenate(hits, axis=0)
        for n in range(0, dext, MXU_DIM):
            cs = slice(n, min(n + MXU_DIM, dext))
            res = jnp.dot(pc, h2e_ref[:, cs], preferred_element_type=F32)
            for e in range(ge):
                cur = xs_ref[e, pl.ds(ws[e], rw), cs].astype(F32)
                xs_ref[e, pl.ds(ws[e], rw), cs] = (cur + res[e * rw:(e + 1) * rw]).astype(BF16)
        return 0

    lax.fori_loop(0, npass, pass_body, 0)

    @pl.when(i == pl.num_programs(2) - 1)
    def _():
        cp = pltpu.make_async_copy(xs_ref, xs_hbm.at[b, pl.ds(g * ge, ge)], sem)
        cp.start()
        cp.wait()


def _ffn_kernel(xs_ref, wg_ref, wu_ref, wd_ref, y_ref, yacc_ref, *, ne, ff_blk):
    e = pl.program_id(1)
    d = wg_ref.shape[0]
    xs = xs_ref[:, 0:d]
    ext = xs_ref[:, d:].astype(F32)
    lane = lax.broadcasted_iota(jnp.int32, ext.shape, 1)
    gate = jnp.sum(jnp.where(jnp.bitwise_and(lane, ne - 1) == e, ext, 0.0), axis=1, keepdims=True)
    ff = wg_ref.shape[1]
    for f in range(ff // ff_blk):
        fs = slice(f * ff_blk, (f + 1) * ff_blk)
        a = jnp.dot(xs, wg_ref[:, fs], preferred_element_type=F32)
        u = jnp.dot(xs, wu_ref[:, fs], preferred_element_type=F32)
        act = (jax.nn.silu(a) * u).astype(BF16)
        contrib = jnp.dot(act, wd_ref[fs, :], preferred_element_type=F32)
        if f == 0:
            yacc_ref[...] = contrib
        else:
            yacc_ref[...] += contrib
    y_ref[...] = (yacc_ref[...] * gate).astype(BF16)


def _combine_kernel(start_ref, hs2_ref, slot_ref, y_hbm, fg_ref, out_ref, ybuf_ref, yextra_ref, sem, sem_x,
                    *, c0, rc, s_rows, nc1):
    b = pl.program_id(0)
    cc = pl.program_id(1)
    n_c = pl.num_programs(1)
    ne = slot_ref.shape[0]
    d = ybuf_ref.shape[3]
    n = b * n_c + cc
    par = n % 2
    r_iota = lax.broadcasted_iota(jnp.int32, (rc, 1), 0)

    def window_lo(bb, chunk):
        base = (bb * nc1 + chunk) * ne
        return [(start_ref[base + e] // BF16_ROWS) * BF16_ROWS for e in range(ne)]

    def window_start(lo_e, q):
        return pl.multiple_of(jnp.minimum(lo_e + q * rc, s_rows - rc), BF16_ROWS)

    def first_copies(bb, chunk, buf):
        lo_ = window_lo(bb, chunk)
        return [pltpu.make_async_copy(y_hbm.at[bb, e, pl.ds(window_start(lo_[e], 0), rc)],
                                      ybuf_ref.at[buf, e], sem.at[buf, e]) for e in range(ne)]

    @pl.when(n == 0)
    def _():
        for cp in first_copies(b, cc + c0, par):
            cp.start()

    @pl.when(n + 1 < pl.num_programs(0) * n_c)
    def _():
        n1 = n + 1
        for cp in first_copies(n1 // n_c, n1 % n_c + c0, 1 - par):
            cp.start()

    c = cc + c0
    base = (b * nc1 + c) * ne
    s1 = [start_ref[base + ne + e] for e in range(ne)]
    lo = window_lo(b, c)
    npass = functools.reduce(jnp.maximum, [(s1[e] - lo[e] + rc - 1) // rc for e in range(ne)])

    def onehots(q):
        hits = []
        for e in range(ne):
            slot_row = slot_ref[e:e + 1, :]
            hit = jnp.logical_and(slot_row == (window_start(lo[e], q) + r_iota), slot_row >= lo[e] + q * rc)
            hits.append(hit.astype(BF16))
        return jnp.concatenate(hits, axis=0)

    def tn_dot(pc, yy):
        return lax.dot_general(pc, yy, (((0,), (0,)), ((), ())), preferred_element_type=F32)

    pc0 = onehots(0)
    for cp in first_copies(b, c, par):
        cp.wait()
    acc0 = tn_dot(pc0, ybuf_ref[par].reshape(ne * rc, d))

    def extra_pass(q, acc):
        cps = [pltpu.make_async_copy(y_hbm.at[b, e, pl.ds(window_start(lo[e], q), rc)],
                                     yextra_ref.at[e], sem_x.at[e]) for e in range(ne)]
        for cp in cps:
            cp.start()
        pc = onehots(q)
        for cp in cps:
            cp.wait()
        return acc + tn_dot(pc, yextra_ref[...].reshape(ne * rc, d))

    acc = lax.fori_loop(1, npass, extra_pass, acc0)
    hs3 = hs2_ref[...] + acc
    ms = jnp.mean(hs3 * hs3, axis=-1, keepdims=True)
    out_ref[...] = hs3 * lax.rsqrt(ms + EPS) * fg_ref[...]


def _vmem(mb):
    return mb * 1024 * 1024


def kernel(x, meta_tokens, norm1_g, w_in, conv_w, conv_b, lru_w_r, lru_b_r, lru_w_i, lru_b_i,
           lru_lambda, lru_out_g, ret_out_g, w_out, norm2_g, w_router, w_gate, w_up, w_down, final_g):
    bsz, seq, d = x.shape
    depth = norm1_g.shape[0]
    assert depth == 1
    n_meta = meta_tokens.shape[0]
    t_len = seq + n_meta
    pad = (-t_len) % CHUNK
    tp = t_len + pad
    nc = tp // CHUNK
    assert (pad + n_meta) % CHUNK == 0
    c0 = (pad + n_meta) // CHUNK
    rows = bsz * tp
    w_lru = conv_w.shape[-1]
    w_ret = ret_out_g.shape[-1]
    in_cols = w_in.shape[-1]
    cb = w_lru
    assert w_ret == cb and in_cols == 6 * cb and cb % MXU_DIM == 0
    hd = w_ret // RET_HEADS
    ne = w_router.shape[-1]
    ff = w_gate.shape[-1]
    cap = EC_CAPACITY * t_len // ne

    tc = _largest_divisor_leq(nc, 5)
    tm = tc * CHUNK
    n_t = nc // tc
    n_tiles = rows // tm

    meta = jnp.broadcast_to(meta_tokens.astype(F32)[None], (bsz, n_meta, d))
    hs = jnp.concatenate([jnp.zeros((bsz, pad, d), F32), meta, x.astype(F32)], axis=1).reshape(rows, d)

    half = hd // 2
    freqs = ROPE_BASE ** (-jnp.arange(half, dtype=F32) / half)
    ang = (jnp.arange(tp) - pad).astype(F32)[:, None] * freqs[None, :]
    cos2 = jnp.concatenate([jnp.cos(ang), jnp.cos(ang)], axis=1)
    sin2 = jnp.concatenate([-jnp.sin(ang), jnp.sin(ang)], axis=1)

    w_in_b = w_in[0].astype(BF16)
    w_out_b = w_out[0].astype(BF16)

    z = pl.pallas_call(
        functools.partial(_inproj_kernel, cb=cb, hd=hd, q_scale=float(hd) ** -0.5),
        grid=(n_tiles,),
        in_specs=[
            pl.BlockSpec((tm, d), lambda i: (i, 0)),
            pl.BlockSpec((1, d), lambda i: (0, 0)),
            pl.BlockSpec((d, in_cols), lambda i: (0, 0)),
            pl.BlockSpec((tm, hd), lambda i: (i % n_t, 0)),
            pl.BlockSpec((tm, hd), lambda i: (i % n_t, 0)),
        ],
        out_specs=pl.BlockSpec((tm, in_cols), lambda i: (i, 0)),
        out_shape=jax.ShapeDtypeStruct((rows, in_cols), F32),
        compiler_params=pltpu.CompilerParams(dimension_semantics=("arbitrary",),
                                             vmem_limit_bytes=_vmem(52)),
        name="inproj",
    )(hs, norm1_g[0][None], w_in_b, cos2, sin2)

    lb = w_lru // LRU_BLOCKS
    per_tile = MXU_DIM // lb
    eye = jnp.eye(per_tile, dtype=F32)

    def blockdiag(wb):
        wb = wb.reshape(2, w_lru // MXU_DIM, per_tile, lb, lb)
        return jnp.einsum('dhpij,pq->dhpiqj', wb, eye).reshape(2, w_lru // MXU_DIM, MXU_DIM, MXU_DIM).astype(BF16)

    wr_bd = blockdiag(lru_w_r[0])
    wi_bd = blockdiag(lru_w_i[0])
    hb8 = tm // SUBLANES
    last8 = rows // SUBLANES - 1

    def tile_of(b, p, j):
        return b * n_t + jnp.where(p == 0, n_t - 1 - j, j)

    y_lru = pl.pallas_call(
        functools.partial(_lru_kernel, n_t=n_t, t_t=tm, pad=pad),
        grid=(bsz, 2, n_t),
        in_specs=[
            pl.BlockSpec((tm, cb), lambda b, p, j: (tile_of(b, p, j), 0)),
            pl.BlockSpec((SUBLANES, cb), lambda b, p, j: (jnp.maximum(tile_of(b, p, j) * hb8 - 1, 0), 0)),
            pl.BlockSpec((SUBLANES, cb), lambda b, p, j: (jnp.minimum((tile_of(b, p, j) + 1) * hb8, last8), 0)),
            pl.BlockSpec((tm, cb), lambda b, p, j: (b * n_t + p * j, 1)),
            pl.BlockSpec((CONV_WIDTH, cb), lambda b, p, j: (0, 0)),
            pl.BlockSpec((1, cb), lambda b, p, j: (0, 0)),
            pl.BlockSpec((None, w_lru // MXU_DIM, MXU_DIM, MXU_DIM), lambda b, p, j: (1 - p, 0, 0, 0)),
            pl.BlockSpec((None, w_lru // MXU_DIM, MXU_DIM, MXU_DIM), lambda b, p, j: (1 - p, 0, 0, 0)),
            pl.BlockSpec((None, 1, cb), lambda b, p, j: (1 - p, 0, 0)),
            pl.BlockSpec((None, 1, cb), lambda b, p, j: (1 - p, 0, 0)),
            pl.BlockSpec((None, 1, cb), lambda b, p, j: (1 - p, 0, 0)),
            pl.BlockSpec((1, cb), lambda b, p, j: (0, 0)),
        ],
        out_specs=pl.BlockSpec((tm, cb), lambda b, p, j: (b * n_t + p * j, 0)),
        out_shape=jax.ShapeDtypeStruct((rows, cb), F32),
        scratch_shapes=[
            pltpu.VMEM((tp, cb), F32),
            pltpu.VMEM((tm, cb), F32),
            pltpu.VMEM((tm, cb), F32),
            pltpu.VMEM((tm + 2 * SUBLANES, cb), F32),
            pltpu.VMEM((1, cb), F32),
        ],
        compiler_params=pltpu.CompilerParams(dimension_semantics=("arbitrary",) * 3,
                                             vmem_limit_bytes=_vmem(52)),
        name="lru",
    )(z, z, z, z, conv_w[0], conv_b[0][None], wr_bd, wi_bd, lru_b_r[0][:, None], lru_b_i[0][:, None],
      lru_lambda[0][:, None], lru_out_g[0][None])

    log_g = jnp.log(1.0 - jnp.exp2(-5.0 - jnp.arange(RET_HEADS, dtype=F32)))
    idx = jnp.arange(CHUNK, dtype=F32)
    dmat = jnp.exp(log_g[:, None, None] * jnp.abs(idx[:, None] - idx[None, :]))

    def posdec(expo):
        return jnp.broadcast_to(jnp.exp(log_g[:, None] * expo[None, :])[:, :, None], (RET_HEADS, CHUNK, hd))

    dvec = jnp.stack([posdec(idx + 1.0), posdec(CHUNK - idx), posdec(CHUNK - 1.0 - idx), posdec(idx)])
    cdm = jnp.broadcast_to(jnp.exp(log_g * CHUNK)[:, None, None], (RET_HEADS, hd, hd))

    y_ret = pl.pallas_call(
        functools.partial(_retention_kernel, n_t=n_t, tc=tc, hd=hd),
        grid=(bsz, 2, n_t),
        in_specs=[
            pl.BlockSpec((tm, cb), lambda b, p, j: (b * n_t + p * j, 2)),
            pl.BlockSpec((tm, cb), lambda b, p, j: (tile_of(b, p, j), 3)),
            pl.BlockSpec((tm, cb), lambda b, p, j: (tile_of(b, p, j), 4)),
            pl.BlockSpec((tm, cb), lambda b, p, j: (b * n_t + p * j, 5)),
            pl.BlockSpec((RET_HEADS, CHUNK, CHUNK), lambda b, p, j: (0, 0, 0)),
            pl.BlockSpec((4, RET_HEADS, CHUNK, hd), lambda b, p, j: (0, 0, 0, 0)),
            pl.BlockSpec((RET_HEADS, hd, hd), lambda b, p, j: (0, 0, 0)),
            pl.BlockSpec((1, cb), lambda b, p, j: (0, 0)),
        ],
        out_specs=pl.BlockSpec((tm, cb), lambda b, p, j: (b * n_t + p * j, 0)),
        out_shape=jax.ShapeDtypeStruct((rows, cb), F32),
        scratch_shapes=[
            pltpu.VMEM((nc, RET_HEADS, hd, hd), BF16),
            pltpu.VMEM((RET_HEADS, hd, hd), F32),
        ],
        compiler_params=pltpu.CompilerParams(dimension_semantics=("arbitrary",) * 3,
                                             vmem_limit_bytes=_vmem(52)),
        name="retention",
    )(z, z, z, z, dmat, dvec, cdm, ret_out_g[0][None])

    assert ne & (ne - 1) == 0 and 3 * ne <= LANES
    dext = d + LANES
    w_router_pad = jnp.pad(w_router[0].astype(F32), ((0, 0), (0, LANES - ne)))
    wr_hi = w_router_pad.astype(BF16)
    wr_mid = (w_router_pad - wr_hi.astype(F32)).astype(BF16)
    hs2, h2e, logits_t = pl.pallas_call(
        functools.partial(_outproj_kernel, ne=ne),
        grid=(bsz, n_t),
        in_specs=[
            pl.BlockSpec((tm, cb), lambda b, j: (b * n_t + j, 0)),
            pl.BlockSpec((tm, cb), lambda b, j: (b * n_t + j, 0)),
            pl.BlockSpec((tm, d), lambda b, j: (b * n_t + j, 0)),
            pl.BlockSpec((d, d), lambda b, j: (0, 0)),
            pl.BlockSpec((1, d), lambda b, j: (0, 0)),
            pl.BlockSpec((d, LANES), lambda b, j: (0, 0)),
            pl.BlockSpec((d, LANES), lambda b, j: (0, 0)),
        ],
        out_specs=[
            pl.BlockSpec((tm, d), lambda b, j: (b * n_t + j, 0)),
            pl.BlockSpec((tm, dext), lambda b, j: (b * n_t + j, 0)),
            pl.BlockSpec((None, ne, tm), lambda b, j: (b, 0, j)),
        ],
        out_shape=[
            jax.ShapeDtypeStruct((rows, d), F32),
            jax.ShapeDtypeStruct((rows, dext), BF16),
            jax.ShapeDtypeStruct((bsz, ne, tp), F32),
        ],
        compiler_params=pltpu.CompilerParams(dimension_semantics=("arbitrary",) * 2,
                                             vmem_limit_bytes=_vmem(52)),
        name="outproj",
    )(y_lru, y_ret, hs, w_out_b, norm2_g[0][None], wr_hi, wr_mid)

    ncp = _round_up(nc, BF16_ROWS)
    logits4 = jnp.pad(logits_t.reshape(bsz, ne, nc, LANES), ((0, 0), (0, 0), (0, ncp - nc), (0, 0)))
    slot4, off4 = pl.pallas_call(
        functools.partial(_route_kernel, pad=pad, tp=tp, cap=cap),
        grid=(bsz,),
        in_specs=[pl.BlockSpec((None, ne, ncp, LANES), lambda b: (b, 0, 0, 0))],
        out_specs=[pl.BlockSpec((None, ne, ncp, LANES), lambda b: (b, 0, 0, 0))] * 2,
        out_shape=[
            jax.ShapeDtypeStruct((bsz, ne, ncp, LANES), jnp.int32),
            jax.ShapeDtypeStruct((bsz, ne, ncp, LANES), jnp.int32),
        ],
        compiler_params=pltpu.CompilerParams(dimension_semantics=("arbitrary",),
                                             vmem_limit_bytes=_vmem(52)),
        name="route",
    )(logits4)

    start = jnp.concatenate([jnp.transpose(off4[:, :, :nc, 0], (0, 2, 1)),
                             jnp.full((bsz, 1, ne), cap, jnp.int32)], axis=1).reshape(-1)
    nc1 = nc + 1
    slot_rows = slot4[:, :, :nc].reshape(bsz, ne, tp)

    s_rows = _round_up(cap, BF16_ROWS)
    tcd = tc
    w_d = tcd * CHUNK
    n_d = nc // tcd
    rw = DISPATCH_WINDOW
    ge = min(ne, SUBLANES)
    assert s_rows >= rw and (s_rows - rw) % BF16_ROWS == 0 and ne % ge == 0
    xs = pl.pallas_call(
        functools.partial(_dispatch_kernel, tcd=tcd, rw=rw, s_rows=s_rows, nc1=nc1, ne=ne),
        grid_spec=pltpu.PrefetchScalarGridSpec(
            num_scalar_prefetch=1,
            grid=(bsz, ne // ge, n_d),
            in_specs=[
                pl.BlockSpec((w_d, dext), lambda b, g, i, st: (b * n_d + i, 0)),
                pl.BlockSpec((None, ge, w_d), lambda b, g, i, st: (b, g, i)),
            ],
            out_specs=pl.BlockSpec(memory_space=pl.ANY),
            scratch_shapes=[
                pltpu.VMEM((ge, s_rows, dext), BF16),
                pltpu.SemaphoreType.DMA,
            ],
        ),
        out_shape=jax.ShapeDtypeStruct((bsz, ne, s_rows, dext), BF16),
        compiler_params=pltpu.CompilerParams(dimension_semantics=("arbitrary",) * 3,
                                             vmem_limit_bytes=_vmem(52)),
        name="dispatch",
    )(start, h2e, slot_rows)

    ff_blk = 512 if ff % 512 == 0 else ff
    y_e = pl.pallas_call(
        functools.partial(_ffn_kernel, ne=ne, ff_blk=ff_blk),
        grid=(bsz, ne),
        in_specs=[
            pl.BlockSpec((None, None, s_rows, dext), lambda b, e: (b, e, 0, 0)),
            pl.BlockSpec((None, d, ff), lambda b, e: (e, 0, 0)),
            pl.BlockSpec((None, d, ff), lambda b, e: (e, 0, 0)),
            pl.BlockSpec((None, ff, d), lambda b, e: (e, 0, 0)),
        ],
        out_specs=pl.BlockSpec((None, None, s_rows, d), lambda b, e: (b, e, 0, 0)),
        out_shape=jax.ShapeDtypeStruct((bsz, ne, s_rows, d), BF16),
        scratch_shapes=[pltpu.VMEM((s_rows, d), F32)],
        compiler_params=pltpu.CompilerParams(dimension_semantics=("arbitrary",) * 2,
                                             vmem_limit_bytes=_vmem(52)),
        name="ffn",
    )(xs, w_gate[0].astype(BF16), w_up[0].astype(BF16), w_down[0].astype(BF16))

    rc = COMBINE_WINDOW
    out = pl.pallas_call(
        functools.partial(_combine_kernel, c0=c0, rc=rc, s_rows=s_rows, nc1=nc1),
        grid_spec=pltpu.PrefetchScalarGridSpec(
            num_scalar_prefetch=1,
            grid=(bsz, nc - c0),
            in_specs=[
                pl.BlockSpec((CHUNK, d), lambda b, c, st: (b * nc + c + c0, 0)),
                pl.BlockSpec((None, None, ne, LANES), lambda b, c, st: (b, c + c0, 0, 0)),
                pl.BlockSpec(memory_space=pl.ANY),
                pl.BlockSpec((1, d), lambda b, c, st: (0, 0)),
            ],
            out_specs=pl.BlockSpec((None, CHUNK, d), lambda b, c, st: (b, c, 0)),
            scratch_shapes=[
                pltpu.VMEM((2, ne, rc, d), BF16),
                pltpu.VMEM((ne, rc, d), BF16),
                pltpu.SemaphoreType.DMA((2, ne)),
                pltpu.SemaphoreType.DMA((ne,)),
            ],
        ),
        out_shape=jax.ShapeDtypeStruct((bsz, seq, d), x.dtype),
        compiler_params=pltpu.CompilerParams(dimension_semantics=("arbitrary",) * 2,
                                             vmem_limit_bytes=_vmem(32)),
        name="combine",
    )(start, hs2, jnp.transpose(slot4[:, :, :nc], (0, 2, 1, 3)), y_e, final_g[None])
    return out
```

```python
import functools

import jax
import jax.numpy as jnp
from jax import lax
from jax.experimental import pallas as pl
from jax.experimental.pallas import tpu as pltpu

N_META = 16
LRU_BLOCKS = 8
LRU_C = 8.0
CONV_WIDTH = 4
CONV_LEFT = 2
RET_HEADS = 4
CHUNK = 128
ROPE_BASE = 10000.0
N_EXPERTS = 16
EC_CAPACITY = 2
EPS = 1e-6

LANES = 128
SUBLANES = 8
BF16_ROWS = 16
MXU_DIM = 256

DISPATCH_WINDOW = 3 * BF16_ROWS
COMBINE_WINDOW = 4 * BF16_ROWS
COMBINE_CHUNKS = 2

F32 = jnp.float32
BF16 = jnp.bfloat16


def _round_up(x, m):
    return (x + m - 1) // m * m


def _largest_divisor_leq(n, k):
    for d in range(min(n, k), 0, -1):
        if n % d == 0:
            return d
    return 1


def _gather_row_tile(head_ref, x_refs, dst_ref):
    first = pl.program_id(1) == 0
    for k, xr in enumerate(x_refs):
        piece = xr[...]
        if k == 0:
            piece = jnp.where(first, head_ref[...], piece)
        dst_ref[k * CHUNK:(k + 1) * CHUNK, :] = piece


def _inproj_kernel(head_ref, *refs, tc, cb, hd, q_scale):
    x_refs = refs[0:tc]
    g_ref, w_ref, cos_ref, sin_ref, z_ref, xt_ref = refs[tc:]
    _gather_row_tile(head_ref, x_refs, xt_ref)
    x = xt_ref[...]
    ms = jnp.mean(x * x, axis=-1, keepdims=True)
    h = (x * lax.rsqrt(ms + EPS) * g_ref[...]).astype(BF16)
    cos = cos_ref[...]
    sin = sin_ref[...]
    for j in range(w_ref.shape[1] // cb):
        zj = jnp.dot(h, w_ref[:, j * cb:(j + 1) * cb], preferred_element_type=F32)
        if j in (2, 3):
            for hh in range(cb // hd):
                t = zj[:, hh * hd:(hh + 1) * hd]
                r = pltpu.roll(t, hd // 2, axis=1)
                t = t * cos + r * sin
                if j == 2:
                    t = t * q_scale
                z_ref[:, j * cb + hh * hd:j * cb + (hh + 1) * hd] = t
        else:
            z_ref[:, j * cb:(j + 1) * cb] = zj


def _scan_tile(a_ref, b_ref, dst_ref, dst_row0, carry_ref, n_rows, forward):
    ng = n_rows // SUBLANES
    row = lax.broadcasted_iota(jnp.int32, (SUBLANES, a_ref.shape[1]), 0)

    def body(g, carry):
        gg = g if forward else ng - 1 - g
        r0 = pl.multiple_of(gg * SUBLANES, SUBLANES)
        a = a_ref[pl.ds(r0, SUBLANES), :]
        b = b_ref[pl.ds(r0, SUBLANES), :]
        for d in (1, 2, 4):
            sh = d if forward else SUBLANES - d
            a_s = pltpu.roll(a, sh, axis=0)
            b_s = pltpu.roll(b, sh, axis=0)
            valid = (row >= d) if forward else (row < SUBLANES - d)
            b = jnp.where(valid, a * b_s + b, b)
            a = jnp.where(valid, a * a_s, a)
        h = a * carry + b
        dst_ref[pl.ds(dst_row0 + r0, SUBLANES), :] = h
        return h[SUBLANES - 1:SUBLANES, :] if forward else h[0:1, :]

    carry_ref[...] = lax.fori_loop(0, ng, body, carry_ref[...])


def _lru_kernel(xm_ref, xp_ref, xn_ref, gate_ref, cw_ref, cb_ref, wr_ref, wi_ref, br_ref, bi_ref,
                lam_ref, og_ref, y_ref, hb_ref, a_ref, b_ref, xpad_ref, carry_ref, *, n_t, t_t, pad):
    p = pl.program_id(1)
    j = pl.program_id(2)
    ti = jnp.where(p == 0, n_t - 1 - j, j)
    w = xm_ref.shape[1]

    @pl.when(j == 0)
    def _():
        carry_ref[...] = jnp.zeros_like(carry_ref)

    xpad_ref[0:SUBLANES, :] = jnp.where(ti == 0, 0.0, xp_ref[...])
    xpad_ref[SUBLANES:SUBLANES + t_t, :] = xm_ref[...]
    xpad_ref[SUBLANES + t_t:, :] = jnp.where(ti == n_t - 1, 0.0, xn_ref[...])
    xp = xpad_ref[...]
    n_pad = t_t + 2 * SUBLANES
    u = jnp.zeros((t_t, w), F32) + cb_ref[...]
    for k in range(CONV_WIDTH):
        off = k - CONV_LEFT
        shifted = xp if off == 0 else pltpu.roll(xp, (-off) % n_pad, axis=0)
        u = u + cw_ref[k:k + 1, :] * shifted[SUBLANES:SUBLANES + t_t, :]

    ub = u.astype(BF16)
    nh = w // MXU_DIM
    pre_r = jnp.concatenate(
        [jnp.dot(ub[:, q * MXU_DIM:(q + 1) * MXU_DIM], wr_ref[q], preferred_element_type=F32)
         for q in range(nh)], axis=1) + br_ref[...]
    pre_i = jnp.concatenate(
        [jnp.dot(ub[:, q * MXU_DIM:(q + 1) * MXU_DIM], wi_ref[q], preferred_element_type=F32)
         for q in range(nh)], axis=1) + bi_ref[...]
    r = 0.5 + 0.5 * jnp.tanh(0.5 * pre_r)
    i = 0.5 + 0.5 * jnp.tanh(0.5 * pre_i)
    nl = -lam_ref[...]
    en = jnp.exp(-jnp.abs(nl))
    w1 = 1.0 + en
    log1p_en = jnp.where(w1 == 1.0, en, en * jnp.log(w1) / jnp.where(w1 == 1.0, 1.0, w1 - 1.0))
    softplus = jnp.maximum(nl, 0.0) + log1p_en
    log_a = (-LRU_C) * r * softplus
    a = jnp.exp(log_a)
    gx = jnp.sqrt(-jnp.tanh(log_a) * (1.0 + a * a)) * (i * u)
    rowidx = ti * t_t + lax.broadcasted_iota(jnp.int32, (t_t, 1), 0)
    gx = jnp.where(jnp.logical_or(p == 0, rowidx >= pad), gx, 0.0)
    a_ref[...] = a
    b_ref[...] = gx

    @pl.when(p == 0)
    def _():
        _scan_tile(a_ref, b_ref, hb_ref, pl.multiple_of(ti * t_t, SUBLANES), carry_ref, t_t, False)

    @pl.when(p == 1)
    def _():
        _scan_tile(a_ref, b_ref, b_ref, 0, carry_ref, t_t, True)
        h = b_ref[...] + hb_ref[pl.ds(pl.multiple_of(ti * t_t, SUBLANES), t_t), :]
        y = h * jax.nn.gelu(gate_ref[...], approximate=True)
        ms = jnp.mean(y * y, axis=-1, keepdims=True)
        y_ref[...] = y * lax.rsqrt(ms + EPS) * og_ref[...]


def _retention_kernel(q_ref, k_ref, v_ref, g_ref, dmat_ref, dvec_ref, cd_ref, og_ref, y_ref,
                      sb_ref, s_ref, *, n_t, tc, hd):
    p = pl.program_id(1)
    j = pl.program_id(2)
    ti = jnp.where(p == 0, n_t - 1 - j, j)
    nheads = q_ref.shape[1] // hd

    @pl.when(j == 0)
    def _():
        s_ref[...] = jnp.zeros_like(s_ref)

    def tn_dot(x, y):
        return lax.dot_general(x, y, (((0,), (0,)), ((), ())), preferred_element_type=F32)

    def nt_dot(x, y):
        return lax.dot_general(x, y, (((1,), (1,)), ((), ())), preferred_element_type=F32)

    @pl.when(p == 0)
    def _():
        for c in reversed(range(tc)):
            cg = ti * tc + c
            for h in range(nheads):
                rs = slice(c * CHUNK, (c + 1) * CHUNK)
                cs = slice(h * hd, (h + 1) * hd)
                kk = k_ref[rs, cs]
                vv = v_ref[rs, cs].astype(BF16)
                sb_ref[cg, h] = s_ref[h].astype(BF16)
                s_ref[h] = s_ref[h] * cd_ref[h] + tn_dot((kk * dvec_ref[3, h]).astype(BF16), vv)

    @pl.when(p == 1)
    def _():
        for c in range(tc):
            cg = ti * tc + c
            for h in range(nheads):
                rs = slice(c * CHUNK, (c + 1) * CHUNK)
                cs = slice(h * hd, (h + 1) * hd)
                qq = q_ref[rs, cs]
                kk = k_ref[rs, cs]
                vv = v_ref[rs, cs].astype(BF16)
                sc = nt_dot(qq.astype(BF16), kk.astype(BF16)) * dmat_ref[h]
                o = jnp.dot(sc.astype(BF16), vv, preferred_element_type=F32)
                o = o + jnp.dot((qq * dvec_ref[0, h]).astype(BF16), s_ref[h].astype(BF16),
                                preferred_element_type=F32)
                o = o + jnp.dot((qq * dvec_ref[1, h]).astype(BF16), sb_ref[cg, h],
                                preferred_element_type=F32)
                s_ref[h] = s_ref[h] * cd_ref[h] + tn_dot((kk * dvec_ref[2, h]).astype(BF16), vv)
                ms = jnp.mean(o * o, axis=-1, keepdims=True)
                o = o * lax.rsqrt(ms + EPS) * og_ref[:, cs]
                y_ref[rs, cs] = o * jax.nn.silu(g_ref[rs, cs])


def _split3(x):
    hi = x.astype(BF16)
    r1 = x - hi.astype(F32)
    mid = r1.astype(BF16)
    lo = (r1 - mid.astype(F32)).astype(BF16)
    return hi, mid, lo


def _outproj_kernel(yl_ref, yr_ref, head_ref, *refs, tc, ne):
    x_refs = refs[0:tc]
    w_ref, g_ref, wr2_ref, hs2_ref, h2e_ref, lg_ref, xt_ref = refs[tc:]
    _gather_row_tile(head_ref, x_refs, xt_ref)
    wl = yl_ref.shape[1]
    d = xt_ref.shape[1]
    mix = jnp.dot(yl_ref[...].astype(BF16), w_ref[0:wl, :], preferred_element_type=F32)
    mix = mix + jnp.dot(yr_ref[...].astype(BF16), w_ref[wl:, :], preferred_element_type=F32)
    hs2 = xt_ref[...] + mix
    hs2_ref[...] = hs2
    ms = jnp.mean(hs2 * hs2, axis=-1, keepdims=True)
    h2 = hs2 * lax.rsqrt(ms + EPS) * g_ref[...]
    hh = h2.astype(BF16)
    h2e_ref[:, 0:d] = hh
    hm = (h2 - hh.astype(F32)).astype(BF16)
    w2 = wr2_ref[...]
    d_hi = jnp.dot(hh, w2, preferred_element_type=F32)
    d_mid = jnp.dot(hm, w2, preferred_element_type=F32)
    lr = d_hi[:, 0:LANES] + (d_hi[:, LANES:] + d_mid[:, 0:LANES])
    lg_ref[...] = lr.T[0:ne, :]

    lane = lax.broadcasted_iota(jnp.int32, lr.shape, 1)
    live = lane < ne
    mx = jnp.max(jnp.where(live, lr, -jnp.inf), axis=1, keepdims=True)
    ex = jnp.where(live, jnp.exp(lr - mx), 0.0)
    aff = ex / jnp.sum(ex, axis=1, keepdims=True)
    a_hi = aff.astype(BF16).astype(F32)
    r1 = aff - a_hi
    a_mid = r1.astype(BF16).astype(F32)
    a_lo = (r1 - a_mid).astype(BF16).astype(F32)
    ext = a_hi + pltpu.roll(a_mid, ne, axis=1) + pltpu.roll(a_lo, 2 * ne, axis=1)
    h2e_ref[:, d:] = ext.astype(BF16)


def _route_kernel(lg_ref, slot_ref, off_ref, *, pad, tp, cap):
    ne, nc, _ = lg_ref.shape
    lg = lg_ref[...]
    m = jnp.max(lg, axis=0, keepdims=True)
    ex = jnp.exp(lg - m)
    aff = ex / jnp.sum(ex, axis=0, keepdims=True)
    tpos = (lax.broadcasted_iota(jnp.int32, (nc, LANES), 0) * LANES
            + lax.broadcasted_iota(jnp.int32, (nc, LANES), 1))
    valid = jnp.logical_and(tpos >= pad, tpos < tp)[None]
    affm = jnp.where(valid, aff, -1.0)

    def count(mask):
        c = jnp.sum(jnp.where(mask, 1.0, 0.0), axis=2, keepdims=True)
        return jnp.sum(c, axis=1, keepdims=True)

    def as_float(bits):
        return lax.bitcast_convert_type(bits, F32)

    def bs_body(_, lohi):
        lo, hi = lohi
        mid = lo + lax.shift_right_logical(hi - lo, 1)
        ok = count(affm >= as_float(mid)) >= float(cap)
        return jnp.where(ok, mid, lo), jnp.where(ok, hi, mid)

    lo0 = jnp.zeros((ne, 1, 1), jnp.int32)
    hi0 = jnp.full((ne, 1, 1), 0x3F800001, jnp.int32)
    thr_bits, _ = lax.fori_loop(0, 31, bs_body, (lo0, hi0))
    thr = as_float(thr_bits)

    gt = affm > thr
    eq = affm == thr
    need = float(cap) - count(gt)

    ci = lax.broadcasted_iota(jnp.int32, (LANES, LANES), 0)
    cj = lax.broadcasted_iota(jnp.int32, (LANES, LANES), 1)
    tri = (ci <= cj).astype(BF16)
    ri = lax.broadcasted_iota(jnp.int32, (nc, nc), 0)
    rj = lax.broadcasted_iota(jnp.int32, (nc, nc), 1)
    ltri = (rj < ri).astype(BF16)

    def prefix(mask2d):
        within = jnp.dot(mask2d.astype(BF16), tri, preferred_element_type=F32)
        tot = jnp.broadcast_to(within[:, LANES - 1:LANES], (nc, LANES)).astype(BF16)
        off = jnp.dot(ltri, tot, preferred_element_type=F32)
        return within + off, off

    for e in range(ne):
        tie_incl, _ = prefix(eq[e])
        tie_excl = tie_incl - eq[e].astype(F32)
        sel = jnp.logical_or(gt[e], jnp.logical_and(eq[e], tie_excl < need[e]))
        cum, off = prefix(sel)
        slot_ref[e] = jnp.where(sel, cum.astype(jnp.int32) - 1, -1)
        off_ref[e] = off.astype(jnp.int32)


def _dispatch_kernel(start_ref, h2e_ref, slot_ref, xs_ref, *, tcd, rw, s_rows, nc1, ne):
    b = pl.program_id(0)
    g = pl.program_id(1)
    i = pl.program_id(2)
    ge = slot_ref.shape[0]
    dext = h2e_ref.shape[1]

    @pl.when(i == 0)
    def _():
        xs_ref[...] = jnp.zeros_like(xs_ref)

    r_iota = lax.broadcasted_iota(jnp.int32, (rw, 1), 0)
    for c in range(tcd):
        base = (b * nc1 + i * tcd + c) * ne + g * ge
        s1 = [start_ref[base + ne + e] for e in range(ge)]
        lo = [(start_ref[base + e] // BF16_ROWS) * BF16_ROWS for e in range(ge)]
        npass = functools.reduce(jnp.maximum, [(s1[e] - lo[e] + rw - 1) // rw for e in range(ge)])
        rows = slice(c * CHUNK, (c + 1) * CHUNK)

        def one_pass(q, lo=lo, rows=rows):
            ws = [pl.multiple_of(jnp.minimum(lo[e] + q * rw, s_rows - rw), BF16_ROWS) for e in range(ge)]
            hits = []
            for e in range(ge):
                slot_row = slot_ref[e:e + 1, rows]
                hit = jnp.logical_and(slot_row == (ws[e] + r_iota), slot_row >= lo[e] + q * rw)
                hits.append(hit.astype(BF16))
            pc = jnp.concatenate(hits, axis=0)
            for n in range(0, dext, MXU_DIM):
                cs = slice(n, min(n + MXU_DIM, dext))
                res = jnp.dot(pc, h2e_ref[rows, cs], preferred_element_type=F32)
                for e in range(ge):
                    xs_ref[e, pl.ds(ws[e], rw), cs] += res[e * rw:(e + 1) * rw].astype(BF16)

        one_pass(0)

        def extra_pass(q, carry, one_pass=one_pass):
            one_pass(q)
            return carry

        lax.fori_loop(1, npass, extra_pass, 0)


def _ffn_kernel(xs_ref, wg_ref, wu_ref, wd_ref, y_ref, wgb_ref, wub_ref, wdb_ref, yacc_ref, *, ne, ff_blk):
    e = pl.program_id(0)

    @pl.when(pl.program_id(1) == 0)
    def _():
        wgb_ref[...] = wg_ref[...].astype(BF16)
        wub_ref[...] = wu_ref[...].astype(BF16)
        wdb_ref[...] = wd_ref[...].astype(BF16)

    wg_ref, wu_ref, wd_ref = wgb_ref, wub_ref, wdb_ref
    d = wg_ref.shape[0]
    xs = xs_ref[:, 0:d]
    ext = xs_ref[:, d:].astype(F32)
    lane = lax.broadcasted_iota(jnp.int32, ext.shape, 1)
    gate = jnp.sum(jnp.where(jnp.bitwise_and(lane, ne - 1) == e, ext, 0.0), axis=1, keepdims=True)
    ff = wg_ref.shape[1]
    for f in range(ff // ff_blk):
        fs = slice(f * ff_blk, (f + 1) * ff_blk)
        a = jnp.dot(xs, wg_ref[:, fs], preferred_element_type=F32)
        u = jnp.dot(xs, wu_ref[:, fs], preferred_element_type=F32)
        act = (jax.nn.silu(a) * u).astype(BF16)
        contrib = jnp.dot(act, wd_ref[fs, :], preferred_element_type=F32)
        if f == 0:
            yacc_ref[...] = contrib
        else:
            yacc_ref[...] += contrib
    y_ref[...] = (yacc_ref[...] * gate).astype(BF16)


def _combine_kernel(start_ref, *refs, c0, cpw, rc, s_rows, nc1):
    hs2_refs = refs[0:cpw]
    slot_refs = refs[cpw:2 * cpw]
    y_hbm, fg_ref, out_ref, ybuf_ref, yextra_ref, sem, sem_x = refs[2 * cpw:]
    b = pl.program_id(0)
    cc = pl.program_id(1)
    n_c = pl.num_programs(1)
    ne = slot_refs[0].shape[0]
    d = ybuf_ref.shape[3]
    n = b * n_c + cc
    par = n % 2
    r_iota = lax.broadcasted_iota(jnp.int32, (rc, 1), 0)

    def window_lo(bb, chunk):
        base = (bb * nc1 + chunk) * ne
        return [(start_ref[base + e] // BF16_ROWS) * BF16_ROWS for e in range(ne)]

    def window_start(lo_e, q):
        return pl.multiple_of(jnp.minimum(lo_e + q * rc, s_rows - rc), BF16_ROWS)

    def first_copies(bb, chunk, buf):
        lo_ = window_lo(bb, chunk)
        return [pltpu.make_async_copy(y_hbm.at[bb, e, pl.ds(window_start(lo_[e], 0), rc)],
                                      ybuf_ref.at[buf, e], sem.at[buf, e]) for e in range(ne)]

    @pl.when(n == 0)
    def _():
        for cp in first_copies(b, cc * cpw + c0, par):
            cp.start()

    @pl.when(n + 1 < pl.num_programs(0) * n_c)
    def _():
        n1 = n + 1
        for cp in first_copies(n1 // n_c, (n1 % n_c) * cpw + c0, 1 - par):
            cp.start()

    c = cc * cpw + c0
    base = (b * nc1 + c) * ne
    s1 = [start_ref[base + cpw * ne + e] for e in range(ne)]
    lo = window_lo(b, c)
    npass = functools.reduce(jnp.maximum, [(s1[e] - lo[e] + rc - 1) // rc for e in range(ne)])

    def onehots(q):
        hits = []
        for e in range(ne):
            slot_row = jnp.concatenate([slot_refs[k][e:e + 1, :] for k in range(cpw)], axis=1)
            hit = jnp.logical_and(slot_row == (window_start(lo[e], q) + r_iota), slot_row >= lo[e] + q * rc)
            hits.append(hit.astype(BF16))
        return jnp.concatenate(hits, axis=0)

    def tn_dot(pc, yy):
        return lax.dot_general(pc, yy, (((0,), (0,)), ((), ())), preferred_element_type=F32)

    pc0 = onehots(0)
    for cp in first_copies(b, c, par):
        cp.wait()
    acc0 = tn_dot(pc0, ybuf_ref[par].reshape(ne * rc, d))

    def extra_pass(q, acc):
        cps = [pltpu.make_async_copy(y_hbm.at[b, e, pl.ds(window_start(lo[e], q), rc)],
                                     yextra_ref.at[e], sem_x.at[e]) for e in range(ne)]
        for cp in cps:
            cp.start()
        pcq = onehots(q)
        for cp in cps:
            cp.wait()
        return acc + tn_dot(pcq, yextra_ref[...].reshape(ne * rc, d))

    acc = lax.fori_loop(1, npass, extra_pass, acc0)
    for k in range(cpw):
        hs3 = hs2_refs[k][...] + acc[k * CHUNK:(k + 1) * CHUNK, :]
        ms = jnp.mean(hs3 * hs3, axis=-1, keepdims=True)
        out_ref[k * CHUNK:(k + 1) * CHUNK, :] = hs3 * lax.rsqrt(ms + EPS) * fg_ref[...]


def _vmem(mb):
    return mb * 1024 * 1024


def kernel(x, meta_tokens, norm1_g, w_in, conv_w, conv_b, lru_w_r, lru_b_r, lru_w_i, lru_b_i,
           lru_lambda, lru_out_g, ret_out_g, w_out, norm2_g, w_router, w_gate, w_up, w_down, final_g):
    bsz, seq, d = x.shape
    depth = norm1_g.shape[0]
    assert depth == 1
    n_meta = meta_tokens.shape[0]
    t_len = seq + n_meta
    pad = (-t_len) % CHUNK
    tp = t_len + pad
    nc = tp // CHUNK
    assert (pad + n_meta) % CHUNK == 0
    c0 = (pad + n_meta) // CHUNK
    rows = bsz * tp
    w_lru = conv_w.shape[-1]
    w_ret = ret_out_g.shape[-1]
    in_cols = w_in.shape[-1]
    cb = w_lru
    assert w_ret == cb and in_cols == 6 * cb and cb % MXU_DIM == 0
    hd = w_ret // RET_HEADS
    ne = w_router.shape[-1]
    ff = w_gate.shape[-1]
    cap = EC_CAPACITY * t_len // ne

    tc = _largest_divisor_leq(nc, 5)
    tm = tc * CHUNK
    n_t = nc // tc
    n_tiles = rows // tm

    assert c0 == 1 and seq % CHUNK == 0
    head = jnp.concatenate([jnp.zeros((pad, d), F32), meta_tokens.astype(F32)], axis=0)
    x4 = x.astype(F32).reshape(bsz, seq // CHUNK, CHUNK, d)

    def x_piece_spec(k):
        return pl.BlockSpec((None, None, CHUNK, d),
                            lambda b, j: (b, jnp.maximum(j * tc + k - 1, 0), 0, 0))

    row_tile_specs = [pl.BlockSpec((CHUNK, d), lambda b, j: (0, 0))] + [x_piece_spec(k) for k in range(tc)]

    half = hd // 2
    freqs = ROPE_BASE ** (-jnp.arange(half, dtype=F32) / half)
    ang = (jnp.arange(tp) - pad).astype(F32)[:, None] * freqs[None, :]
    cos2 = jnp.concatenate([jnp.cos(ang), jnp.cos(ang)], axis=1)
    sin2 = jnp.concatenate([-jnp.sin(ang), jnp.sin(ang)], axis=1)

    w_in_b = w_in[0].astype(BF16)
    w_out_b = w_out[0].astype(BF16)

    z = pl.pallas_call(
        functools.partial(_inproj_kernel, tc=tc, cb=cb, hd=hd, q_scale=float(hd) ** -0.5),
        grid=(bsz, n_t),
        in_specs=row_tile_specs + [
            pl.BlockSpec((1, d), lambda b, j: (0, 0)),
            pl.BlockSpec((d, in_cols), lambda b, j: (0, 0)),
            pl.BlockSpec((tm, hd), lambda b, j: (j, 0)),
            pl.BlockSpec((tm, hd), lambda b, j: (j, 0)),
        ],
        out_specs=pl.BlockSpec((tm, in_cols), lambda b, j: (b * n_t + j, 0)),
        out_shape=jax.ShapeDtypeStruct((rows, in_cols), F32),
        scratch_shapes=[pltpu.VMEM((tm, d), F32)],
        compiler_params=pltpu.CompilerParams(dimension_semantics=("arbitrary",) * 2,
                                             vmem_limit_bytes=_vmem(52)),
        name="inproj",
    )(head, *([x4] * tc), norm1_g[0][None], w_in_b, cos2, sin2)

    lb = w_lru // LRU_BLOCKS
    per_tile = MXU_DIM // lb
    eye = jnp.eye(per_tile, dtype=F32)

    def blockdiag(wb):
        wb = wb.reshape(2, w_lru // MXU_DIM, per_tile, lb, lb)
        return jnp.einsum('dhpij,pq->dhpiqj', wb, eye).reshape(2, w_lru // MXU_DIM, MXU_DIM, MXU_DIM).astype(BF16)

    wr_bd = blockdiag(lru_w_r[0])
    wi_bd = blockdiag(lru_w_i[0])
    hb8 = tm // SUBLANES
    last8 = rows // SUBLANES - 1

    def tile_of(b, p, j):
        return b * n_t + jnp.where(p == 0, n_t - 1 - j, j)

    y_lru = pl.pallas_call(
        functools.partial(_lru_kernel, n_t=n_t, t_t=tm, pad=pad),
        grid=(bsz, 2, n_t),
        in_specs=[
            pl.BlockSpec((tm, cb), lambda b, p, j: (tile_of(b, p, j), 0)),
            pl.BlockSpec((SUBLANES, cb), lambda b, p, j: (jnp.maximum(tile_of(b, p, j) * hb8 - 1, 0), 0)),
            pl.BlockSpec((SUBLANES, cb), lambda b, p, j: (jnp.minimum((tile_of(b, p, j) + 1) * hb8, last8), 0)),
            pl.BlockSpec((tm, cb), lambda b, p, j: (b * n_t + p * j, 1)),
            pl.BlockSpec((CONV_WIDTH, cb), lambda b, p, j: (0, 0)),
            pl.BlockSpec((1, cb), lambda b, p, j: (0, 0)),
            pl.BlockSpec((None, w_lru // MXU_DIM, MXU_DIM, MXU_DIM), lambda b, p, j: (1 - p, 0, 0, 0)),
            pl.BlockSpec((None, w_lru // MXU_DIM, MXU_DIM, MXU_DIM), lambda b, p, j: (1 - p, 0, 0, 0)),
            pl.BlockSpec((None, 1, cb), lambda b, p, j: (1 - p, 0, 0)),
            pl.BlockSpec((None, 1, cb), lambda b, p, j: (1 - p, 0, 0)),
            pl.BlockSpec((None, 1, cb), lambda b, p, j: (1 - p, 0, 0)),
            pl.BlockSpec((1, cb), lambda b, p, j: (0, 0)),
        ],
        out_specs=pl.BlockSpec((tm, cb), lambda b, p, j: (b * n_t + p * j, 0)),
        out_shape=jax.ShapeDtypeStruct((rows, cb), F32),
        scratch_shapes=[
            pltpu.VMEM((tp, cb), F32),
            pltpu.VMEM((tm, cb), F32),
            pltpu.VMEM((tm, cb), F32),
            pltpu.VMEM((tm + 2 * SUBLANES, cb), F32),
            pltpu.VMEM((1, cb), F32),
        ],
        compiler_params=pltpu.CompilerParams(dimension_semantics=("arbitrary",) * 3,
                                             vmem_limit_bytes=_vmem(52)),
        name="lru",
    )(z, z, z, z, conv_w[0], conv_b[0][None], wr_bd, wi_bd, lru_b_r[0][:, None], lru_b_i[0][:, None],
      lru_lambda[0][:, None], lru_out_g[0][None])

    log_g = jnp.log(1.0 - jnp.exp2(-5.0 - jnp.arange(RET_HEADS, dtype=F32)))
    idx = jnp.arange(CHUNK, dtype=F32)
    dmat = jnp.exp(log_g[:, None, None] * jnp.abs(idx[:, None] - idx[None, :]))

    def posdec(expo):
        return jnp.broadcast_to(jnp.exp(log_g[:, None] * expo[None, :])[:, :, None], (RET_HEADS, CHUNK, hd))

    dvec = jnp.stack([posdec(idx + 1.0), posdec(CHUNK - idx), posdec(CHUNK - 1.0 - idx), posdec(idx)])
    cdm = jnp.broadcast_to(jnp.exp(log_g * CHUNK)[:, None, None], (RET_HEADS, hd, hd))

    y_ret = pl.pallas_call(
        functools.partial(_retention_kernel, n_t=n_t, tc=tc, hd=hd),
        grid=(bsz, 2, n_t),
        in_specs=[
            pl.BlockSpec((tm, cb), lambda b, p, j: (b * n_t + p * j, 2)),
            pl.BlockSpec((tm, cb), lambda b, p, j: (tile_of(b, p, j), 3)),
            pl.BlockSpec((tm, cb), lambda b, p, j: (tile_of(b, p, j), 4)),
            pl.BlockSpec((tm, cb), lambda b, p, j: (b * n_t + p * j, 5)),
            pl.BlockSpec((RET_HEADS, CHUNK, CHUNK), lambda b, p, j: (0, 0, 0)),
            pl.BlockSpec((4, RET_HEADS, CHUNK, hd), lambda b, p, j: (0, 0, 0, 0)),
            pl.BlockSpec((RET_HEADS, hd, hd), lambda b, p, j: (0, 0, 0)),
            pl.BlockSpec((1, cb), lambda b, p, j: (0, 0)),
        ],
        out_specs=pl.BlockSpec((tm, cb), lambda b, p, j: (b * n_t + p * j, 0)),
        out_shape=jax.ShapeDtypeStruct((rows, cb), F32),
        scratch_shapes=[
            pltpu.VMEM((nc, RET_HEADS, hd, hd), BF16),
            pltpu.VMEM((RET_HEADS, hd, hd), F32),
        ],
        compiler_params=pltpu.CompilerParams(dimension_semantics=("arbitrary",) * 3,
                                             vmem_limit_bytes=_vmem(52)),
        name="retention",
    )(z, z, z, z, dmat, dvec, cdm, ret_out_g[0][None])

    assert ne & (ne - 1) == 0 and 3 * ne <= LANES
    dext = d + LANES
    w_router_pad = jnp.pad(w_router[0].astype(F32), ((0, 0), (0, LANES - ne)))
    wr_hi = w_router_pad.astype(BF16)
    wr_mid = (w_router_pad - wr_hi.astype(F32)).astype(BF16)
    hs2, h2e, logits_t = pl.pallas_call(
        functools.partial(_outproj_kernel, tc=tc, ne=ne),
        grid=(bsz, n_t),
        in_specs=[
            pl.BlockSpec((tm, cb), lambda b, j: (b * n_t + j, 0)),
            pl.BlockSpec((tm, cb), lambda b, j: (b * n_t + j, 0)),
        ] + row_tile_specs + [
            pl.BlockSpec((d, d), lambda b, j: (0, 0)),
            pl.BlockSpec((1, d), lambda b, j: (0, 0)),
            pl.BlockSpec((d, 2 * LANES), lambda b, j: (0, 0)),
        ],
        out_specs=[
            pl.BlockSpec((tm, d), lambda b, j: (b * n_t + j, 0)),
            pl.BlockSpec((tm, dext), lambda b, j: (b * n_t + j, 0)),
            pl.BlockSpec((None, ne, tm), lambda b, j: (b, 0, j)),
        ],
        out_shape=[
            jax.ShapeDtypeStruct((rows, d), F32),
            jax.ShapeDtypeStruct((rows, dext), BF16),
            jax.ShapeDtypeStruct((bsz, ne, tp), F32),
        ],
        scratch_shapes=[pltpu.VMEM((tm, d), F32)],
        compiler_params=pltpu.CompilerParams(dimension_semantics=("arbitrary",) * 2,
                                             vmem_limit_bytes=_vmem(52)),
        name="outproj",
    )(y_lru, y_ret, head, *([x4] * tc), w_out_b, norm2_g[0][None],
      jnp.concatenate([wr_hi, wr_mid], axis=1))

    ncp = _round_up(nc, BF16_ROWS)
    logits4 = jnp.pad(logits_t.reshape(bsz, ne, nc, LANES), ((0, 0), (0, 0), (0, ncp - nc), (0, 0)))
    slot4, off4 = pl.pallas_call(
        functools.partial(_route_kernel, pad=pad, tp=tp, cap=cap),
        grid=(bsz,),
        in_specs=[pl.BlockSpec((None, ne, ncp, LANES), lambda b: (b, 0, 0, 0))],
        out_specs=[pl.BlockSpec((None, ne, ncp, LANES), lambda b: (b, 0, 0, 0))] * 2,
        out_shape=[
            jax.ShapeDtypeStruct((bsz, ne, ncp, LANES), jnp.int32),
            jax.ShapeDtypeStruct((bsz, ne, ncp, LANES), jnp.int32),
        ],
        compiler_params=pltpu.CompilerParams(dimension_semantics=("arbitrary",),
                                             vmem_limit_bytes=_vmem(52)),
        name="route",
    )(logits4)

    start = jnp.concatenate([jnp.transpose(off4[:, :, :nc, 0], (0, 2, 1)),
                             jnp.full((bsz, 1, ne), cap, jnp.int32)], axis=1).reshape(-1)
    nc1 = nc + 1
    slot_rows = slot4[:, :, :nc].reshape(bsz, ne, tp)

    s_rows = _round_up(cap, BF16_ROWS)
    tcd = tc
    w_d = tcd * CHUNK
    n_d = nc // tcd
    rw = DISPATCH_WINDOW
    ge = min(ne, SUBLANES)
    assert s_rows >= rw and (s_rows - rw) % BF16_ROWS == 0 and ne % ge == 0
    xs = pl.pallas_call(
        functools.partial(_dispatch_kernel, tcd=tcd, rw=rw, s_rows=s_rows, nc1=nc1, ne=ne),
        grid_spec=pltpu.PrefetchScalarGridSpec(
            num_scalar_prefetch=1,
            grid=(bsz, ne // ge, n_d),
            in_specs=[
                pl.BlockSpec((w_d, dext), lambda b, g, i, st: (b * n_d + i, 0)),
                pl.BlockSpec((None, ge, w_d), lambda b, g, i, st: (b, g, i)),
            ],
            out_specs=pl.BlockSpec((None, ge, s_rows, dext), lambda b, g, i, st: (b, g, 0, 0)),
        ),
        out_shape=jax.ShapeDtypeStruct((bsz, ne, s_rows, dext), BF16),
        compiler_params=pltpu.CompilerParams(dimension_semantics=("arbitrary",) * 3,
                                             vmem_limit_bytes=_vmem(52)),
        name="dispatch",
    )(start, h2e, slot_rows)

    ff_blk = 512 if ff % 512 == 0 else ff
    y_e = pl.pallas_call(
        functools.partial(_ffn_kernel, ne=ne, ff_blk=ff_blk),
        grid=(ne, bsz),
        in_specs=[
            pl.BlockSpec((None, None, s_rows, dext), lambda e, b: (b, e, 0, 0)),
            pl.BlockSpec((None, d, ff), lambda e, b: (e, 0, 0)),
            pl.BlockSpec((None, d, ff), lambda e, b: (e, 0, 0)),
            pl.BlockSpec((None, ff, d), lambda e, b: (e, 0, 0)),
        ],
        out_specs=pl.BlockSpec((None, None, s_rows, d), lambda e, b: (b, e, 0, 0)),
        out_shape=jax.ShapeDtypeStruct((bsz, ne, s_rows, d), BF16),
        scratch_shapes=[
            pltpu.VMEM((d, ff), BF16),
            pltpu.VMEM((d, ff), BF16),
            pltpu.VMEM((ff, d), BF16),
            pltpu.VMEM((s_rows, d), F32),
        ],
        compiler_params=pltpu.CompilerParams(dimension_semantics=("arbitrary",) * 2,
                                             vmem_limit_bytes=_vmem(56)),
        name="ffn",
    )(xs, w_gate[0], w_up[0], w_down[0])

    rc = COMBINE_WINDOW
    cpw = COMBINE_CHUNKS if (nc - c0) % COMBINE_CHUNKS == 0 else 1
    assert (s_rows - rc) % BF16_ROWS == 0
    slot_t = jnp.transpose(slot4[:, :, :nc], (0, 2, 1, 3))
    out = pl.pallas_call(
        functools.partial(_combine_kernel, c0=c0, cpw=cpw, rc=rc, s_rows=s_rows, nc1=nc1),
        grid_spec=pltpu.PrefetchScalarGridSpec(
            num_scalar_prefetch=1,
            grid=(bsz, (nc - c0) // cpw),
            in_specs=(
                [pl.BlockSpec((CHUNK, d), functools.partial(
                    lambda b, c, st, k: (b * nc + c * cpw + c0 + k, 0), k=k)) for k in range(cpw)]
                + [pl.BlockSpec((None, None, ne, LANES), functools.partial(
                    lambda b, c, st, k: (b, c * cpw + c0 + k, 0, 0), k=k)) for k in range(cpw)]
                + [pl.BlockSpec(memory_space=pl.ANY),
                   pl.BlockSpec((1, d), lambda b, c, st: (0, 0))]),
            out_specs=pl.BlockSpec((None, cpw * CHUNK, d), lambda b, c, st: (b, c, 0)),
            scratch_shapes=[
                pltpu.VMEM((2, ne, rc, d), BF16),
                pltpu.VMEM((ne, rc, d), BF16),
                pltpu.SemaphoreType.DMA((2, ne)),
                pltpu.SemaphoreType.DMA((ne,)),
            ],
        ),
        out_shape=jax.ShapeDtypeStruct((bsz, seq, d), x.dtype),
        compiler_params=pltpu.CompilerParams(dimension_semantics=("arbitrary",) * 2,
                                             vmem_limit_bytes=_vmem(32)),
        name="combine",
    )(start, *([hs2] * cpw), *([slot_t] * cpw), y_e, final_g[None])
    return out
```

```python
import functools

import jax
import jax.numpy as jnp
from jax import lax
from jax.experimental import pallas as pl
from jax.experimental.pallas import tpu as pltpu

N_META = 16
LRU_BLOCKS = 8
LRU_C = 8.0
CONV_WIDTH = 4
CONV_LEFT = 2
RET_HEADS = 4
CHUNK = 128
ROPE_BASE = 10000.0
N_EXPERTS = 16
EC_CAPACITY = 2
EPS = 1e-6

LANES = 128
SUBLANES = 8
BF16_ROWS = 16
MXU_DIM = 256

DISPATCH_WINDOW = 3 * BF16_ROWS
COMBINE_WINDOW = 4 * BF16_ROWS
COMBINE_CHUNKS = 2

F32 = jnp.float32
BF16 = jnp.bfloat16


def _round_up(x, m):
    return (x + m - 1) // m * m


def _largest_divisor_leq(n, k):
    for d in range(min(n, k), 0, -1):
        if n % d == 0:
            return d
    return 1


def _gather_row_tile(head_ref, x_refs, dst_ref):
    first = pl.program_id(1) == 0
    for k, xr in enumerate(x_refs):
        piece = xr[...]
        if k == 0:
            piece = jnp.where(first, head_ref[...], piece)
        dst_ref[k * CHUNK:(k + 1) * CHUNK, :] = piece


def _inproj_kernel(head_ref, *refs, tc, cb, hd, q_scale):
    x_refs = refs[0:tc]
    g_ref, w_ref, cos_ref, sin_ref, z_ref, xt_ref = refs[tc:]
    _gather_row_tile(head_ref, x_refs, xt_ref)
    x = xt_ref[...]
    ms = jnp.mean(x * x, axis=-1, keepdims=True)
    h = (x * lax.rsqrt(ms + EPS) * g_ref[...]).astype(BF16)
    cos = cos_ref[...]
    sin = sin_ref[...]
    for j in range(w_ref.shape[1] // cb):
        zj = jnp.dot(h, w_ref[:, j * cb:(j + 1) * cb], preferred_element_type=F32)
        if j in (2, 3):
            for hh in range(cb // hd):
                t = zj[:, hh * hd:(hh + 1) * hd]
                r = pltpu.roll(t, hd // 2, axis=1)
                t = t * cos + r * sin
                if j == 2:
                    t = t * q_scale
                z_ref[:, j * cb + hh * hd:j * cb + (hh + 1) * hd] = t.astype(z_ref.dtype)
        else:
            z_ref[:, j * cb:(j + 1) * cb] = zj.astype(z_ref.dtype)


def _interleave_in(src_ref, slab_ref, dst_ref, seg, off):
    nsl = slab_ref.shape[0]
    for c in range(nsl):
        slab_ref[c] = src_ref[:, c * LANES:(c + 1) * LANES].astype(F32)
    for j in range(seg):
        dst_ref[off + j] = jnp.concatenate(
            [slab_ref[c, pl.ds(j, SUBLANES, stride=seg), :] for c in range(nsl)], axis=1)


def _interleave_out(src_ref, slab_ref, dst_ref, seg):
    nsl = slab_ref.shape[0]
    for j in range(seg):
        row = src_ref[j]
        for c in range(nsl):
            slab_ref[c, pl.ds(j, SUBLANES, stride=seg), :] = row[:, c * LANES:(c + 1) * LANES]
    for c in range(nsl):
        dst_ref[:, c * LANES:(c + 1) * LANES] = slab_ref[c].astype(dst_ref.dtype)


def _scan_tile(a_ref, b_ref, h_ref, acc_ref, carry_ref, seg, forward):
    w = a_ref.shape[2]

    def body(jj, carry):
        h, acc = carry
        j = jj if forward else seg - 1 - jj
        a = a_ref[j]
        h = a * h + b_ref[j]
        acc = a * acc
        h_ref[j] = h
        acc_ref[j] = acc
        return h, acc

    h_end, a_end = lax.fori_loop(0, seg, body, (jnp.zeros((SUBLANES, w), F32), jnp.ones((SUBLANES, w), F32)))
    c = carry_ref[...]
    rows = [None] * SUBLANES
    order = range(SUBLANES) if forward else range(SUBLANES - 1, -1, -1)
    for s in order:
        rows[s] = c
        c = a_end[s:s + 1, :] * c + h_end[s:s + 1, :]
    carry_ref[...] = c
    cvec = jnp.concatenate(rows, axis=0)
    return h_ref[...] + acc_ref[...] * cvec[None]


def _lru_kernel(xm_ref, xp_ref, xn_ref, gate_ref, cw_ref, cb_ref, wr_ref, wi_ref, br_ref, bi_ref,
                lam_ref, og_ref, y_ref, hb_ref, slab_ref, xext_ref, a_ref, b_ref, h_ref, acc_ref, carry_ref,
                *, n_t, t_t, pad):
    p = pl.program_id(1)
    j = pl.program_id(2)
    ti = jnp.where(p == 0, n_t - 1 - j, j)
    w = xm_ref.shape[1]
    seg = t_t // SUBLANES

    @pl.when(j == 0)
    def _():
        carry_ref[...] = jnp.zeros_like(carry_ref)

    _interleave_in(xm_ref, slab_ref, xext_ref, seg, CONV_LEFT)
    sub = lax.broadcasted_iota(jnp.int32, (SUBLANES, w), 0)
    nhalo = xp_ref.shape[0]
    xp8 = jnp.where(ti == 0, 0.0, xp_ref[...].astype(F32))
    xn8 = jnp.where(ti == n_t - 1, 0.0, xn_ref[...].astype(F32))
    for back in range(1, CONV_LEFT + 1):
        xext_ref[CONV_LEFT - back] = jnp.where(
            sub == 0, xp8[nhalo - back:nhalo - back + 1, :],
            pltpu.roll(xext_ref[CONV_LEFT + seg - back], 1, axis=0))
    for fwd in range(CONV_WIDTH - 1 - CONV_LEFT):
        xext_ref[CONV_LEFT + seg + fwd] = jnp.where(
            sub == SUBLANES - 1, xn8[fwd:fwd + 1, :],
            pltpu.roll(xext_ref[CONV_LEFT + fwd], SUBLANES - 1, axis=0))
    u3 = jnp.zeros((seg, SUBLANES, w), F32) + cb_ref[...]
    for k in range(CONV_WIDTH):
        u3 = u3 + cw_ref[k:k + 1, :] * xext_ref[k:k + seg]
    u = u3.reshape(t_t, w)

    ub = u.astype(BF16)
    nh = w // MXU_DIM
    pre_r = jnp.concatenate(
        [jnp.dot(ub[:, q * MXU_DIM:(q + 1) * MXU_DIM], wr_ref[q], preferred_element_type=F32)
         for q in range(nh)], axis=1) + br_ref[...]
    pre_i = jnp.concatenate(
        [jnp.dot(ub[:, q * MXU_DIM:(q + 1) * MXU_DIM], wi_ref[q], preferred_element_type=F32)
         for q in range(nh)], axis=1) + bi_ref[...]
    tr = jnp.tanh(pre_r)
    i = 0.5 + 0.5 * jnp.tanh(pre_i)
    nl = -lam_ref[...]
    en = jnp.exp(-jnp.abs(nl))
    w1 = 1.0 + en
    log1p_en = jnp.where(w1 == 1.0, en, en * jnp.log(w1) / jnp.where(w1 == 1.0, 1.0, w1 - 1.0))
    half_c = (-0.5 * LRU_C) * (jnp.maximum(nl, 0.0) + log1p_en)
    log_a = half_c + half_c * tr
    a = jnp.exp(log_a)
    v = -jnp.tanh(log_a) * (1.0 + a * a)
    gx = jnp.where(v > 0.0, v * lax.rsqrt(v), 0.0) * (i * u)
    a_ref[...] = a.reshape(seg, SUBLANES, w)
    b_ref[...] = gx.reshape(seg, SUBLANES, w)

    @pl.when(jnp.logical_and(p == 1, ti * t_t < pad))
    def _():
        j3 = lax.broadcasted_iota(jnp.int32, (seg, SUBLANES, 1), 0)
        s3 = lax.broadcasted_iota(jnp.int32, (seg, SUBLANES, 1), 1)
        rowidx = ti * t_t + s3 * seg + j3
        b_ref[...] = jnp.where(rowidx >= pad, b_ref[...], 0.0)

    @pl.when(p == 0)
    def _():
        hb_ref[ti] = _scan_tile(a_ref, b_ref, h_ref, acc_ref, carry_ref, seg, False)

    @pl.when(p == 1)
    def _():
        h = _scan_tile(a_ref, b_ref, h_ref, acc_ref, carry_ref, seg, True) + hb_ref[ti]
        _interleave_in(gate_ref, slab_ref, b_ref, seg, 0)
        y = h * jax.nn.gelu(b_ref[...], approximate=True)
        ms = jnp.mean(y * y, axis=-1, keepdims=True)
        h_ref[...] = y * lax.rsqrt(ms + EPS) * og_ref[...]
        _interleave_out(h_ref, slab_ref, y_ref, seg)


def _retention_kernel(q_ref, k_ref, v_ref, g_ref, dmat_ref, dvec_ref, cd_ref, og_ref, y_ref,
                      sb_ref, s_ref, *, n_t, tc, hd):
    p = pl.program_id(1)
    j = pl.program_id(2)
    ti = jnp.where(p == 0, n_t - 1 - j, j)
    nheads = q_ref.shape[1] // hd

    @pl.when(j == 0)
    def _():
        s_ref[...] = jnp.zeros_like(s_ref)

    def tn_dot(x, y):
        return lax.dot_general(x, y, (((0,), (0,)), ((), ())), preferred_element_type=F32)

    def nt_dot(x, y):
        return lax.dot_general(x, y, (((1,), (1,)), ((), ())), preferred_element_type=F32)

    @pl.when(p == 0)
    def _():
        for c in reversed(range(tc)):
            cg = ti * tc + c
            for h in range(nheads):
                rs = slice(c * CHUNK, (c + 1) * CHUNK)
                cs = slice(h * hd, (h + 1) * hd)
                kk = k_ref[rs, cs].astype(F32)
                vv = v_ref[rs, cs].astype(BF16)
                sb_ref[cg, h] = s_ref[h].astype(BF16)
                s_ref[h] = s_ref[h] * cd_ref[h] + tn_dot((kk * dvec_ref[3, h]).astype(BF16), vv)

    @pl.when(p == 1)
    def _():
        for c in range(tc):
            cg = ti * tc + c
            for h in range(nheads):
                rs = slice(c * CHUNK, (c + 1) * CHUNK)
                cs = slice(h * hd, (h + 1) * hd)
                qq = q_ref[rs, cs].astype(F32)
                kk = k_ref[rs, cs].astype(F32)
                vv = v_ref[rs, cs].astype(BF16)
                sc = nt_dot(qq.astype(BF16), kk.astype(BF16)) * dmat_ref[h]
                o = jnp.dot(sc.astype(BF16), vv, preferred_element_type=F32)
                o = o + jnp.dot((qq * dvec_ref[0, h]).astype(BF16), s_ref[h].astype(BF16),
                                preferred_element_type=F32)
                o = o + jnp.dot((qq * dvec_ref[1, h]).astype(BF16), sb_ref[cg, h],
                                preferred_element_type=F32)
                s_ref[h] = s_ref[h] * cd_ref[h] + tn_dot((kk * dvec_ref[2, h]).astype(BF16), vv)
                ms = jnp.mean(o * o, axis=-1, keepdims=True)
                o = o * lax.rsqrt(ms + EPS) * og_ref[:, cs]
                y_ref[rs, cs] = (o * jax.nn.silu(g_ref[rs, cs].astype(F32))).astype(y_ref.dtype)


def _split3(x):
    hi = x.astype(BF16)
    r1 = x - hi.astype(F32)
    mid = r1.astype(BF16)
    lo = (r1 - mid.astype(F32)).astype(BF16)
    return hi, mid, lo


def _outproj_kernel(yl_ref, yr_ref, head_ref, *refs, tc, ne):
    x_refs = refs[0:tc]
    w_ref, g_ref, wr2_ref, hs2_ref, h2e_ref, lg_ref, xt_ref = refs[tc:]
    _gather_row_tile(head_ref, x_refs, xt_ref)
    wl = yl_ref.shape[1]
    d = xt_ref.shape[1]
    mix = jnp.dot(yl_ref[...].astype(BF16), w_ref[0:wl, :], preferred_element_type=F32)
    mix = mix + jnp.dot(yr_ref[...].astype(BF16), w_ref[wl:, :], preferred_element_type=F32)
    hs2 = xt_ref[...] + mix
    hs2_ref[...] = hs2
    ms = jnp.mean(hs2 * hs2, axis=-1, keepdims=True)
    h2 = hs2 * lax.rsqrt(ms + EPS) * g_ref[...]
    hh = h2.astype(BF16)
    h2e_ref[:, 0:d] = hh
    hm = (h2 - hh.astype(F32)).astype(BF16)
    w2 = wr2_ref[...]
    d_hi = jnp.dot(hh, w2, preferred_element_type=F32)
    d_mid = jnp.dot(hm, w2, preferred_element_type=F32)
    lr = d_hi[:, 0:LANES] + (d_hi[:, LANES:] + d_mid[:, 0:LANES])
    lg_ref[...] = lr.T[0:ne, :]

    lane = lax.broadcasted_iota(jnp.int32, lr.shape, 1)
    live = lane < ne
    mx = jnp.max(jnp.where(live, lr, -jnp.inf), axis=1, keepdims=True)
    ex = jnp.where(live, jnp.exp(lr - mx), 0.0)
    aff = ex / jnp.sum(ex, axis=1, keepdims=True)
    a_hi = aff.astype(BF16).astype(F32)
    r1 = aff - a_hi
    a_mid = r1.astype(BF16).astype(F32)
    a_lo = (r1 - a_mid).astype(BF16).astype(F32)
    ext = a_hi + pltpu.roll(a_mid, ne, axis=1) + pltpu.roll(a_lo, 2 * ne, axis=1)
    h2e_ref[:, d:] = ext.astype(BF16)


def _route_kernel(lg_ref, slot_ref, off_ref, *, pad, tp, cap):
    ne, nc, _ = lg_ref.shape
    lg = lg_ref[...]
    m = jnp.max(lg, axis=0, keepdims=True)
    ex = jnp.exp(lg - m)
    aff = ex / jnp.sum(ex, axis=0, keepdims=True)
    tpos = (lax.broadcasted_iota(jnp.int32, (nc, LANES), 0) * LANES
            + lax.broadcasted_iota(jnp.int32, (nc, LANES), 1))
    valid = jnp.logical_and(tpos >= pad, tpos < tp)[None]
    affm = jnp.where(valid, aff, -1.0)

    def count(mask):
        c = jnp.sum(jnp.where(mask, 1.0, 0.0), axis=1, keepdims=True)
        return jnp.sum(c, axis=2, keepdims=True)

    def as_float(bits):
        return lax.bitcast_convert_type(bits, F32)

    def bs_body(_, lohi):
        lo, hi = lohi
        mid = lo + lax.shift_right_logical(hi - lo, 1)
        ok = count(affm >= as_float(mid)) >= float(cap)
        return jnp.where(ok, mid, lo), jnp.where(ok, hi, mid)

    lo0 = jnp.zeros((ne, 1, 1), jnp.int32)
    hi0 = jnp.full((ne, 1, 1), 0x3F800001, jnp.int32)
    thr_bits, _ = lax.fori_loop(0, 31, bs_body, (lo0, hi0))
    thr = as_float(thr_bits)

    gt = affm > thr
    eq = affm == thr
    need = float(cap) - count(gt)

    ci = lax.broadcasted_iota(jnp.int32, (LANES, LANES), 0)
    cj = lax.broadcasted_iota(jnp.int32, (LANES, LANES), 1)
    tri = (ci <= cj).astype(BF16)
    ri = lax.broadcasted_iota(jnp.int32, (nc, nc), 0)
    rj = lax.broadcasted_iota(jnp.int32, (nc, nc), 1)
    ltri = (rj < ri).astype(BF16)

    def prefix(mask2d):
        within = jnp.dot(mask2d.astype(BF16), tri, preferred_element_type=F32)
        tot = jnp.broadcast_to(within[:, LANES - 1:LANES], (nc, LANES)).astype(BF16)
        off = jnp.dot(ltri, tot, preferred_element_type=F32)
        return within + off, off

    for e in range(ne):
        tie_incl, _ = prefix(eq[e])
        tie_excl = tie_incl - eq[e].astype(F32)
        sel = jnp.logical_or(gt[e], jnp.logical_and(eq[e], tie_excl < need[e]))
        cum, off = prefix(sel)
        slot_ref[e] = jnp.where(sel, cum.astype(jnp.int32) - 1, -1)
        off_ref[e] = off.astype(jnp.int32)


def _dispatch_kernel(start_ref, h2e_ref, slot_ref, xs_ref, *, tcd, rw, s_rows, nc1, ne):
    b = pl.program_id(0)
    g = pl.program_id(1)
    i = pl.program_id(2)
    ge = slot_ref.shape[0]
    dext = h2e_ref.shape[1]

    @pl.when(i == 0)
    def _():
        xs_ref[...] = jnp.zeros_like(xs_ref)

    r_iota = lax.broadcasted_iota(jnp.int32, (rw, 1), 0)
    for c in range(tcd):
        base = (b * nc1 + i * tcd + c) * ne + g * ge
        s1 = [start_ref[base + ne + e] for e in range(ge)]
        lo = [(start_ref[base + e] // BF16_ROWS) * BF16_ROWS for e in range(ge)]
        npass = functools.reduce(jnp.maximum, [(s1[e] - lo[e] + rw - 1) // rw for e in range(ge)])
        rows = slice(c * CHUNK, (c + 1) * CHUNK)

        def one_pass(q, lo=lo, rows=rows):
            ws = [pl.multiple_of(jnp.minimum(lo[e] + q * rw, s_rows - rw), BF16_ROWS) for e in range(ge)]
            hits = []
            for e in range(ge):
                slot_row = slot_ref[e:e + 1, rows]
                hit = jnp.logical_and(slot_row == (ws[e] + r_iota), slot_row >= lo[e] + q * rw)
                hits.append(hit.astype(BF16))
            pc = jnp.concatenate(hits, axis=0)
            for n in range(0, dext, MXU_DIM):
                cs = slice(n, min(n + MXU_DIM, dext))
                res = jnp.dot(pc, h2e_ref[rows, cs], preferred_element_type=F32)
                for e in range(ge):
                    xs_ref[e, pl.ds(ws[e], rw), cs] += res[e * rw:(e + 1) * rw].astype(BF16)

        one_pass(0)

        def extra_pass(q, carry, one_pass=one_pass):
            one_pass(q)
            return carry

        lax.fori_loop(1, npass, extra_pass, 0)


def _ffn_kernel(xs_ref, wg_ref, wu_ref, wd_ref, y_ref, wgb_ref, wub_ref, wdb_ref, yacc_ref, *, ne, ff_blk):
    e = pl.program_id(0)

    @pl.when(pl.program_id(1) == 0)
    def _():
        wgb_ref[...] = wg_ref[...].astype(BF16)
        wub_ref[...] = wu_ref[...].astype(BF16)
        wdb_ref[...] = wd_ref[...].astype(BF16)

    wg_ref, wu_ref, wd_ref = wgb_ref, wub_ref, wdb_ref
    d = wg_ref.shape[0]
    xs = xs_ref[:, 0:d]
    ext = xs_ref[:, d:].astype(F32)
    lane = lax.broadcasted_iota(jnp.int32, ext.shape, 1)
    gate = jnp.sum(jnp.where(jnp.bitwise_and(lane, ne - 1) == e, ext, 0.0), axis=1, keepdims=True)
    ff = wg_ref.shape[1]
    for f in range(ff // ff_blk):
        fs = slice(f * ff_blk, (f + 1) * ff_blk)
        a = jnp.dot(xs, wg_ref[:, fs], preferred_element_type=F32)
        u = jnp.dot(xs, wu_ref[:, fs], preferred_element_type=F32)
        act = (jax.nn.silu(a) * u).astype(BF16)
        contrib = jnp.dot(act, wd_ref[fs, :], preferred_element_type=F32)
        if f == 0:
            yacc_ref[...] = contrib
        else:
            yacc_ref[...] += contrib
    y_ref[...] = (yacc_ref[...] * gate).astype(BF16)


def _combine_kernel(start_ref, *refs, c0, cpw, rc, s_rows, nc1):
    hs2_refs = refs[0:cpw]
    slot_refs = refs[cpw:2 * cpw]
    y_hbm, fg_ref, out_ref, ybuf_ref, yextra_ref, sem, sem_x = refs[2 * cpw:]
    b = pl.program_id(0)
    cc = pl.program_id(1)
    n_c = pl.num_programs(1)
    ne = slot_refs[0].shape[0]
    d = ybuf_ref.shape[3]
    n = b * n_c + cc
    par = n % 2
    r_iota = lax.broadcasted_iota(jnp.int32, (rc, 1), 0)

    def window_lo(bb, chunk):
        base = (bb * nc1 + chunk) * ne
        return [(start_ref[base + e] // BF16_ROWS) * BF16_ROWS for e in range(ne)]

    def window_start(lo_e, q):
        return pl.multiple_of(jnp.minimum(lo_e + q * rc, s_rows - rc), BF16_ROWS)

    def first_copies(bb, chunk, buf):
        lo_ = window_lo(bb, chunk)
        return [pltpu.make_async_copy(y_hbm.at[bb, e, pl.ds(window_start(lo_[e], 0), rc)],
                                      ybuf_ref.at[buf, e], sem.at[buf, e]) for e in range(ne)]

    @pl.when(n == 0)
    def _():
        for cp in first_copies(b, cc * cpw + c0, par):
            cp.start()

    @pl.when(n + 1 < pl.num_programs(0) * n_c)
    def _():
        n1 = n + 1
        for cp in first_copies(n1 // n_c, (n1 % n_c) * cpw + c0, 1 - par):
            cp.start()

    c = cc * cpw + c0
    base = (b * nc1 + c) * ne
    s1 = [start_ref[base + cpw * ne + e] for e in range(ne)]
    lo = window_lo(b, c)
    npass = functools.reduce(jnp.maximum, [(s1[e] - lo[e] + rc - 1) // rc for e in range(ne)])

    def onehots(q):
        hits = []
        for e in range(ne):
            slot_row = jnp.concatenate([slot_refs[k][e:e + 1, :] for k in range(cpw)], axis=1)
            hit = jnp.logical_and(slot_row == (window_start(lo[e], q) + r_iota), slot_row >= lo[e] + q * rc)
            hits.append(hit.astype(BF16))
        return jnp.concatenate(hits, axis=0)

    def tn_dot(pc, yy):
        return lax.dot_general(pc, yy, (((0,), (0,)), ((), ())), preferred_element_type=F32)

    pc0 = onehots(0)
    for cp in first_copies(b, c, par):
        cp.wait()
    acc0 = tn_dot(pc0, ybuf_ref[par].reshape(ne * rc, d))

    def extra_pass(q, acc):
        cps = [pltpu.make_async_copy(y_hbm.at[b, e, pl.ds(window_start(lo[e], q), rc)],
                                     yextra_ref.at[e], sem_x.at[e]) for e in range(ne)]
        for cp in cps:
            cp.start()
        pcq = onehots(q)
        for cp in cps:
            cp.wait()
        return acc + tn_dot(pcq, yextra_ref[...].reshape(ne * rc, d))

    acc = lax.fori_loop(1, npass, extra_pass, acc0)
    for k in range(cpw):
        hs3 = hs2_refs[k][...] + acc[k * CHUNK:(k + 1) * CHUNK, :]
        ms = jnp.mean(hs3 * hs3, axis=-1, keepdims=True)
        out_ref[k * CHUNK:(k + 1) * CHUNK, :] = hs3 * lax.rsqrt(ms + EPS) * fg_ref[...]


def _vmem(mb):
    return mb * 1024 * 1024


def kernel(x, meta_tokens, norm1_g, w_in, conv_w, conv_b, lru_w_r, lru_b_r, lru_w_i, lru_b_i,
           lru_lambda, lru_out_g, ret_out_g, w_out, norm2_g, w_router, w_gate, w_up, w_down, final_g):
    bsz, seq, d = x.shape
    depth = norm1_g.shape[0]
    assert depth == 1
    n_meta = meta_tokens.shape[0]
    t_len = seq + n_meta
    pad = (-t_len) % CHUNK
    tp = t_len + pad
    nc = tp // CHUNK
    assert (pad + n_meta) % CHUNK == 0
    c0 = (pad + n_meta) // CHUNK
    rows = bsz * tp
    w_lru = conv_w.shape[-1]
    w_ret = ret_out_g.shape[-1]
    in_cols = w_in.shape[-1]
    cb = w_lru
    assert w_ret == cb and in_cols == 6 * cb and cb % MXU_DIM == 0
    hd = w_ret // RET_HEADS
    ne = w_router.shape[-1]
    ff = w_gate.shape[-1]
    cap = EC_CAPACITY * t_len // ne

    tc = _largest_divisor_leq(nc, 5)
    tm = tc * CHUNK
    n_t = nc // tc
    n_tiles = rows // tm

    assert c0 == 1 and seq % CHUNK == 0
    head = jnp.concatenate([jnp.zeros((pad, d), F32), meta_tokens.astype(F32)], axis=0)
    x4 = x.astype(F32).reshape(bsz, seq // CHUNK, CHUNK, d)

    def x_piece_spec(k):
        return pl.BlockSpec((None, None, CHUNK, d),
                            lambda b, j: (b, jnp.maximum(j * tc + k - 1, 0), 0, 0))

    row_tile_specs = [pl.BlockSpec((CHUNK, d), lambda b, j: (0, 0))] + [x_piece_spec(k) for k in range(tc)]

    half = hd // 2
    freqs = ROPE_BASE ** (-jnp.arange(half, dtype=F32) / half)
    ang = (jnp.arange(tp) - pad).astype(F32)[:, None] * freqs[None, :]
    cos2 = jnp.concatenate([jnp.cos(ang), jnp.cos(ang)], axis=1)
    sin2 = jnp.concatenate([-jnp.sin(ang), jnp.sin(ang)], axis=1)

    w_in_b = w_in[0].astype(BF16)
    w_out_b = w_out[0].astype(BF16)

    z = pl.pallas_call(
        functools.partial(_inproj_kernel, tc=tc, cb=cb, hd=hd, q_scale=float(hd) ** -0.5),
        grid=(bsz, n_t),
        in_specs=row_tile_specs + [
            pl.BlockSpec((1, d), lambda b, j: (0, 0)),
            pl.BlockSpec((d, in_cols), lambda b, j: (0, 0)),
            pl.BlockSpec((tm, hd), lambda b, j: (j, 0)),
            pl.BlockSpec((tm, hd), lambda b, j: (j, 0)),
        ],
        out_specs=pl.BlockSpec((tm, in_cols), lambda b, j: (b * n_t + j, 0)),
        out_shape=jax.ShapeDtypeStruct((rows, in_cols), BF16),
        scratch_shapes=[pltpu.VMEM((tm, d), F32)],
        compiler_params=pltpu.CompilerParams(dimension_semantics=("arbitrary",) * 2,
                                             vmem_limit_bytes=_vmem(52)),
        name="inproj",
    )(head, *([x4] * tc), norm1_g[0][None], w_in_b, cos2, sin2)

    lb = w_lru // LRU_BLOCKS
    per_tile = MXU_DIM // lb
    eye = jnp.eye(per_tile, dtype=F32)

    def blockdiag(wb):
        wb = wb.reshape(2, w_lru // MXU_DIM, per_tile, lb, lb)
        return jnp.einsum('dhpij,pq->dhpiqj', wb, eye).reshape(2, w_lru // MXU_DIM, MXU_DIM, MXU_DIM).astype(BF16)

    wr_bd = blockdiag(0.5 * lru_w_r[0])
    wi_bd = blockdiag(0.5 * lru_w_i[0])
    def tile_of(b, p, j):
        return b * n_t + jnp.where(p == 0, n_t - 1 - j, j)

    t_l = max(t for t in range(32, min(tp, 512) + 1, 32) if tp % t == 0 and (t // 32) % 2 == 1)
    n_l = tp // t_l
    seg = t_l // SUBLANES
    assert seg >= CONV_WIDTH and cb % LANES == 0
    hb8 = t_l // BF16_ROWS
    last8 = rows // BF16_ROWS - 1

    def ltile_of(b, p, j):
        return b * n_l + jnp.where(p == 0, n_l - 1 - j, j)

    y_lru = pl.pallas_call(
        functools.partial(_lru_kernel, n_t=n_l, t_t=t_l, pad=pad),
        grid=(bsz, 2, n_l),
        in_specs=[
            pl.BlockSpec((t_l, cb), lambda b, p, j: (ltile_of(b, p, j), 0)),
            pl.BlockSpec((BF16_ROWS, cb), lambda b, p, j: (jnp.maximum(ltile_of(b, p, j) * hb8 - 1, 0), 0)),
            pl.BlockSpec((BF16_ROWS, cb), lambda b, p, j: (jnp.minimum((ltile_of(b, p, j) + 1) * hb8, last8), 0)),
            pl.BlockSpec((t_l, cb), lambda b, p, j: (b * n_l + p * j, 1)),
            pl.BlockSpec((CONV_WIDTH, cb), lambda b, p, j: (0, 0)),
            pl.BlockSpec((1, cb), lambda b, p, j: (0, 0)),
            pl.BlockSpec((None, w_lru // MXU_DIM, MXU_DIM, MXU_DIM), lambda b, p, j: (1 - p, 0, 0, 0)),
            pl.BlockSpec((None, w_lru // MXU_DIM, MXU_DIM, MXU_DIM), lambda b, p, j: (1 - p, 0, 0, 0)),
            pl.BlockSpec((None, 1, cb), lambda b, p, j: (1 - p, 0, 0)),
            pl.BlockSpec((None, 1, cb), lambda b, p, j: (1 - p, 0, 0)),
            pl.BlockSpec((None, 1, cb), lambda b, p, j: (1 - p, 0, 0)),
            pl.BlockSpec((1, cb), lambda b, p, j: (0, 0)),
        ],
        out_specs=pl.BlockSpec((t_l, cb), lambda b, p, j: (b * n_l + p * j, 0)),
        out_shape=jax.ShapeDtypeStruct((rows, cb), BF16),
        scratch_shapes=[
            pltpu.VMEM((n_l, seg, SUBLANES, cb), F32),
            pltpu.VMEM((cb // LANES, t_l, LANES), F32),
            pltpu.VMEM((seg + CONV_WIDTH - 1, SUBLANES, cb), F32),
            pltpu.VMEM((seg, SUBLANES, cb), F32),
            pltpu.VMEM((seg, SUBLANES, cb), F32),
            pltpu.VMEM((seg, SUBLANES, cb), F32),
            pltpu.VMEM((seg, SUBLANES, cb), F32),
            pltpu.VMEM((1, cb), F32),
        ],
        compiler_params=pltpu.CompilerParams(dimension_semantics=("arbitrary",) * 3,
                                             vmem_limit_bytes=_vmem(52)),
        name="lru",
    )(z, z, z, z, conv_w[0], conv_b[0][None], wr_bd, wi_bd, 0.5 * lru_b_r[0][:, None], 0.5 * lru_b_i[0][:, None],
      lru_lambda[0][:, None], lru_out_g[0][None])

    log_g = jnp.log(1.0 - jnp.exp2(-5.0 - jnp.arange(RET_HEADS, dtype=F32)))
    idx = jnp.arange(CHUNK, dtype=F32)
    dmat = jnp.exp(log_g[:, None, None] * jnp.abs(idx[:, None] - idx[None, :]))

    def posdec(expo):
        return jnp.broadcast_to(jnp.exp(log_g[:, None] * expo[None, :])[:, :, None], (RET_HEADS, CHUNK, hd))

    dvec = jnp.stack([posdec(idx + 1.0), posdec(CHUNK - idx), posdec(CHUNK - 1.0 - idx), posdec(idx)])
    cdm = jnp.broadcast_to(jnp.exp(log_g * CHUNK)[:, None, None], (RET_HEADS, hd, hd))

    y_ret = pl.pallas_call(
        functools.partial(_retention_kernel, n_t=n_t, tc=tc, hd=hd),
        grid=(bsz, 2, n_t),
        in_specs=[
            pl.BlockSpec((tm, cb), lambda b, p, j: (b * n_t + p * j, 2)),
            pl.BlockSpec((tm, cb), lambda b, p, j: (tile_of(b, p, j), 3)),
            pl.BlockSpec((tm, cb), lambda b, p, j: (tile_of(b, p, j), 4)),
            pl.BlockSpec((tm, cb), lambda b, p, j: (b * n_t + p * j, 5)),
            pl.BlockSpec((RET_HEADS, CHUNK, CHUNK), lambda b, p, j: (0, 0, 0)),
            pl.BlockSpec((4, RET_HEADS, CHUNK, hd), lambda b, p, j: (0, 0, 0, 0)),
            pl.BlockSpec((RET_HEADS, hd, hd), lambda b, p, j: (0, 0, 0)),
            pl.BlockSpec((1, cb), lambda b, p, j: (0, 0)),
        ],
        out_specs=pl.BlockSpec((tm, cb), lambda b, p, j: (b * n_t + p * j, 0)),
        out_shape=jax.ShapeDtypeStruct((rows, cb), BF16),
        scratch_shapes=[
            pltpu.VMEM((nc, RET_HEADS, hd, hd), BF16),
            pltpu.VMEM((RET_HEADS, hd, hd), F32),
        ],
        compiler_params=pltpu.CompilerParams(dimension_semantics=("arbitrary",) * 3,
                                             vmem_limit_bytes=_vmem(52)),
        name="retention",
    )(z, z, z, z, dmat, dvec, cdm, ret_out_g[0][None])

    assert ne & (ne - 1) == 0 and 3 * ne <= LANES
    dext = d + LANES
    w_router_pad = jnp.pad(w_router[0].astype(F32), ((0, 0), (0, LANES - ne)))
    wr_hi = w_router_pad.astype(BF16)
    wr_mid = (w_router_pad - wr_hi.astype(F32)).astype(BF16)
    hs2, h2e, logits_t = pl.pallas_call(
        functools.partial(_outproj_kernel, tc=tc, ne=ne),
        grid=(bsz, n_t),
        in_specs=[
            pl.BlockSpec((tm, cb), lambda b, j: (b * n_t + j, 0)),
            pl.BlockSpec((tm, cb), lambda b, j: (b * n_t + j, 0)),
        ] + row_tile_specs + [
            pl.BlockSpec((d, d), lambda b, j: (0, 0)),
            pl.BlockSpec((1, d), lambda b, j: (0, 0)),
            pl.BlockSpec((d, 2 * LANES), lambda b, j: (0, 0)),
        ],
        out_specs=[
            pl.BlockSpec((tm, d), lambda b, j: (b * n_t + j, 0)),
            pl.BlockSpec((tm, dext), lambda b, j: (b * n_t + j, 0)),
            pl.BlockSpec((None, ne, tm), lambda b, j: (b, 0, j)),
        ],
        out_shape=[
            jax.ShapeDtypeStruct((rows, d), F32),
            jax.ShapeDtypeStruct((rows, dext), BF16),
            jax.ShapeDtypeStruct((bsz, ne, tp), F32),
        ],
        scratch_shapes=[pltpu.VMEM((tm, d), F32)],
        compiler_params=pltpu.CompilerParams(dimension_semantics=("arbitrary",) * 2,
                                             vmem_limit_bytes=_vmem(52)),
        name="outproj",
    )(y_lru, y_ret, head, *([x4] * tc), w_out_b, norm2_g[0][None],
      jnp.concatenate([wr_hi, wr_mid], axis=1))

    ncp = _round_up(nc, BF16_ROWS)
    logits4 = jnp.pad(logits_t.reshape(bsz, ne, nc, LANES), ((0, 0), (0, 0), (0, ncp - nc), (0, 0)))
    slot4, off4 = pl.pallas_call(
        functools.partial(_route_kernel, pad=pad, tp=tp, cap=cap),
        grid=(bsz,),
        in_specs=[pl.BlockSpec((None, ne, ncp, LANES), lambda b: (b, 0, 0, 0))],
        out_specs=[pl.BlockSpec((None, ne, ncp, LANES), lambda b: (b, 0, 0, 0))] * 2,
        out_shape=[
            jax.ShapeDtypeStruct((bsz, ne, ncp, LANES), jnp.int32),
            jax.ShapeDtypeStruct((bsz, ne, ncp, LANES), jnp.int32),
        ],
        compiler_params=pltpu.CompilerParams(dimension_semantics=("arbitrary",),
                                             vmem_limit_bytes=_vmem(52)),
        name="route",
    )(logits4)

    start = jnp.concatenate([jnp.transpose(off4[:, :, :nc, 0], (0, 2, 1)),
                             jnp.full((bsz, 1, ne), cap, jnp.int32)], axis=1).reshape(-1)
    nc1 = nc + 1
    slot_rows = slot4[:, :, :nc].reshape(bsz, ne, tp)

    s_rows = _round_up(cap, BF16_ROWS)
    tcd = tc
    w_d = tcd * CHUNK
    n_d = nc // tcd
    rw = DISPATCH_WINDOW
    ge = min(ne, SUBLANES)
    assert s_rows >= rw and (s_rows - rw) % BF16_ROWS == 0 and ne % ge == 0
    xs = pl.pallas_call(
        functools.partial(_dispatch_kernel, tcd=tcd, rw=rw, s_rows=s_rows, nc1=nc1, ne=ne),
        grid_spec=pltpu.PrefetchScalarGridSpec(
            num_scalar_prefetch=1,
            grid=(bsz, ne // ge, n_d),
            in_specs=[
                pl.BlockSpec((w_d, dext), lambda b, g, i, st: (b * n_d + i, 0)),
                pl.BlockSpec((None, ge, w_d), lambda b, g, i, st: (b, g, i)),
            ],
            out_specs=pl.BlockSpec((None, ge, s_rows, dext), lambda b, g, i, st: (b, g, 0, 0)),
        ),
        out_shape=jax.ShapeDtypeStruct((bsz, ne, s_rows, dext), BF16),
        compiler_params=pltpu.CompilerParams(dimension_semantics=("arbitrary",) * 3,
                                             vmem_limit_bytes=_vmem(52)),
        name="dispatch",
    )(start, h2e, slot_rows)

    ff_blk = 512 if ff % 512 == 0 else ff
    y_e = pl.pallas_call(
        functools.partial(_ffn_kernel, ne=ne, ff_blk=ff_blk),
        grid=(ne, bsz),
        in_specs=[
            pl.BlockSpec((None, None, s_rows, dext), lambda e, b: (b, e, 0, 0)),
            pl.BlockSpec((None, d, ff), lambda e, b: (e, 0, 0)),
            pl.BlockSpec((None, d, ff), lambda e, b: (e, 0, 0)),
            pl.BlockSpec((None, ff, d), lambda e, b: (e, 0, 0)),
        ],
        out_specs=pl.BlockSpec((None, None, s_rows, d), lambda e, b: (b, e, 0, 0)),
        out_shape=jax.ShapeDtypeStruct((bsz, ne, s_rows, d), BF16),
        scratch_shapes=[
            pltpu.VMEM((d, ff), BF16),
            pltpu.VMEM((d, ff), BF16),
            pltpu.VMEM((ff, d), BF16),
            pltpu.VMEM((s_rows, d), F32),
        ],
        compiler_params=pltpu.CompilerParams(dimension_semantics=("arbitrary",) * 2,
                                             vmem_limit_bytes=_vmem(56)),
        name="ffn",
    )(xs, w_gate[0], w_up[0], w_down[0])

    rc = COMBINE_WINDOW
    cpw = COMBINE_CHUNKS if (nc - c0) % COMBINE_CHUNKS == 0 else 1
    assert (s_rows - rc) % BF16_ROWS == 0
    slot_t = jnp.transpose(slot4[:, :, :nc], (0, 2, 1, 3))
    out = pl.pallas_call(
        functools.partial(_combine_kernel, c0=c0, cpw=cpw, rc=rc, s_rows=s_rows, nc1=nc1),
        grid_spec=pltpu.PrefetchScalarGridSpec(
            num_scalar_prefetch=1,
            grid=(bsz, (nc - c0) // cpw),
            in_specs=(
                [pl.BlockSpec((CHUNK, d), functools.partial(
                    lambda b, c, st, k: (b * nc + c * cpw + c0 + k, 0), k=k)) for k in range(cpw)]
                + [pl.BlockSpec((None, None, ne, LANES), functools.partial(
                    lambda b, c, st, k: (b, c * cpw + c0 + k, 0, 0), k=k)) for k in range(cpw)]
                + [pl.BlockSpec(memory_space=pl.ANY),
                   pl.BlockSpec((1, d), lambda b, c, st: (0, 0))]),
            out_specs=pl.BlockSpec((None, cpw * CHUNK, d), lambda b, c, st: (b, c, 0)),
            scratch_shapes=[
                pltpu.VMEM((2, ne, rc, d), BF16),
                pltpu.VMEM((ne, rc, d), BF16),
                pltpu.SemaphoreType.DMA((2, ne)),
                pltpu.SemaphoreType.DMA((ne,)),
            ],
        ),
        out_shape=jax.ShapeDtypeStruct((bsz, seq, d), x.dtype),
        compiler_params=pltpu.CompilerParams(dimension_semantics=("arbitrary",) * 2,
                                             vmem_limit_bytes=_vmem(32)),
        name="combine",
    )(start, *([hs2] * cpw), *([slot_t] * cpw), y_e, final_g[None])
    return out
```

```python
import functools

import jax
import jax.numpy as jnp
from jax import lax
from jax.experimental import pallas as pl
from jax.experimental.pallas import tpu as pltpu

N_META = 16
LRU_BLOCKS = 8
LRU_C = 8.0
CONV_WIDTH = 4
CONV_LEFT = 2
RET_HEADS = 4
CHUNK = 128
ROPE_BASE = 10000.0
N_EXPERTS = 16
EC_CAPACITY = 2
EPS = 1e-6

LANES = 128
SUBLANES = 8
BF16_ROWS = 16
MXU_DIM = 256

DISPATCH_WINDOW = 3 * BF16_ROWS
DISPATCH_WINDOW_PAIR = 4 * BF16_ROWS
COMBINE_WINDOW = 4 * BF16_ROWS
COMBINE_CHUNKS = 2

F32 = jnp.float32
BF16 = jnp.bfloat16


def _round_up(x, m):
    return (x + m - 1) // m * m


def _largest_divisor_leq(n, k):
    for d in range(min(n, k), 0, -1):
        if n % d == 0:
            return d
    return 1


def _gather_row_tile(head_ref, x_refs, dst_ref):
    first = pl.program_id(1) == 0
    for k, xr in enumerate(x_refs):
        piece = xr[...]
        if k == 0:
            piece = jnp.where(first, head_ref[...], piece)
        dst_ref[k * CHUNK:(k + 1) * CHUNK, :] = piece


def _inproj_kernel(head_ref, *refs, tc, cb, hd, q_scale):
    x_refs = refs[0:tc]
    g_ref, w_ref, cos_ref, sin_ref, z_ref, xt_ref = refs[tc:]
    _gather_row_tile(head_ref, x_refs, xt_ref)
    x = xt_ref[...]
    ms = jnp.mean(x * x, axis=-1, keepdims=True)
    h = (x * lax.rsqrt(ms + EPS) * g_ref[...]).astype(BF16)
    cos = cos_ref[...]
    sin = sin_ref[...]
    for j in range(w_ref.shape[1] // cb):
        zj = jnp.dot(h, w_ref[:, j * cb:(j + 1) * cb], preferred_element_type=F32)
        if j in (2, 3):
            for hh in range(cb // hd):
                t = zj[:, hh * hd:(hh + 1) * hd]
                r = pltpu.roll(t, hd // 2, axis=1)
                t = t * cos + r * sin
                if j == 2:
                    t = t * q_scale
                z_ref[:, j * cb + hh * hd:j * cb + (hh + 1) * hd] = t.astype(z_ref.dtype)
        else:
            z_ref[:, j * cb:(j + 1) * cb] = zj.astype(z_ref.dtype)


def _interleave_in(src_ref, r0, slab_ref, dst_ref, seg, off):
    nsl, t_t = slab_ref.shape[0], slab_ref.shape[1]
    for c in range(nsl):
        slab_ref[c] = src_ref[pl.ds(r0, t_t), c * LANES:(c + 1) * LANES].astype(F32)
    for j in range(seg):
        dst_ref[off + j] = jnp.concatenate(
            [slab_ref[c, pl.ds(j, SUBLANES, stride=seg), :] for c in range(nsl)], axis=1)


def _interleave_out(src_ref, slab_ref, dst_ref, r0, seg):
    nsl, t_t = slab_ref.shape[0], slab_ref.shape[1]
    for j in range(seg):
        row = src_ref[j]
        for c in range(nsl):
            slab_ref[c, pl.ds(j, SUBLANES, stride=seg), :] = row[:, c * LANES:(c + 1) * LANES]
    for c in range(nsl):
        dst_ref[pl.ds(r0, t_t), c * LANES:(c + 1) * LANES] = slab_ref[c].astype(dst_ref.dtype)


def _scan_tile(a_ref, b_ref, h_ref, acc_ref, carry_ref, seg, forward):
    w = a_ref.shape[2]

    def body(jj, carry):
        h, acc = carry
        j = jj if forward else seg - 1 - jj
        a = a_ref[j]
        h = a * h + b_ref[j]
        acc = a * acc
        h_ref[j] = h
        acc_ref[j] = acc
        return h, acc

    h_end, a_end = lax.fori_loop(0, seg, body, (jnp.zeros((SUBLANES, w), F32), jnp.ones((SUBLANES, w), F32)))
    c = carry_ref[...]
    rows = [None] * SUBLANES
    order = range(SUBLANES) if forward else range(SUBLANES - 1, -1, -1)
    for s in order:
        rows[s] = c
        c = a_end[s:s + 1, :] * c + h_end[s:s + 1, :]
    carry_ref[...] = c
    cvec = jnp.concatenate(rows, axis=0)
    return h_ref[...] + acc_ref[...] * cvec[None]


def _lru_kernel(xm_ref, xp_ref, xn_ref, gate_ref, cw_ref, cb_ref, wr_ref, wi_ref, br_ref, bi_ref,
                lam_ref, og_ref, y_ref, hb_ref, slab_ref, xext_ref, a_ref, b_ref, h_ref, acc_ref, carry_ref,
                *, n_t, t_t, pad):
    p = pl.program_id(1)

    @pl.when(pl.program_id(2) == 0)
    def _():
        carry_ref[...] = jnp.zeros_like(carry_ref)

    nsub = xm_ref.shape[0] // t_t

    def sub_tile(k, carry):
        kk = jnp.where(p == 0, nsub - 1 - k, k)
        blk = jnp.where(p == 0, pl.num_programs(2) - 1 - pl.program_id(2), pl.program_id(2))
        _lru_tile(kk, blk * nsub + kk, xm_ref, xp_ref, xn_ref, gate_ref, cw_ref, cb_ref, wr_ref, wi_ref,
                  br_ref, bi_ref, lam_ref, og_ref, y_ref, hb_ref, slab_ref, xext_ref, a_ref, b_ref, h_ref,
                  acc_ref, carry_ref, n_t=n_t, t_t=t_t, pad=pad)
        return carry

    lax.fori_loop(0, nsub, sub_tile, 0)


def _lru_tile(kk, ti, xm_ref, xp_ref, xn_ref, gate_ref, cw_ref, cb_ref, wr_ref, wi_ref, br_ref, bi_ref,
              lam_ref, og_ref, y_ref, hb_ref, slab_ref, xext_ref, a_ref, b_ref, h_ref, acc_ref, carry_ref,
              *, n_t, t_t, pad):
    p = pl.program_id(1)
    w = xm_ref.shape[1]
    seg = t_t // SUBLANES
    nsub = xm_ref.shape[0] // t_t
    nhalo = xp_ref.shape[0]
    r0 = pl.multiple_of(kk * t_t, BF16_ROWS)

    _interleave_in(xm_ref, r0, slab_ref, xext_ref, seg, CONV_LEFT)
    sub = lax.broadcasted_iota(jnp.int32, (SUBLANES, w), 0)
    before = xm_ref[pl.ds(pl.multiple_of(jnp.maximum(r0 - nhalo, 0), BF16_ROWS), nhalo), :]
    after = xm_ref[pl.ds(pl.multiple_of(jnp.minimum(r0 + t_t, (nsub - 1) * t_t), BF16_ROWS), nhalo), :]
    xp8 = jnp.where(kk == 0, xp_ref[...], before).astype(F32)
    xn8 = jnp.where(kk == nsub - 1, xn_ref[...], after).astype(F32)
    xp8 = jnp.where(ti == 0, 0.0, xp8)
    xn8 = jnp.where(ti == n_t - 1, 0.0, xn8)
    for back in range(1, CONV_LEFT + 1):
        xext_ref[CONV_LEFT - back] = jnp.where(
            sub == 0, xp8[nhalo - back:nhalo - back + 1, :],
            pltpu.roll(xext_ref[CONV_LEFT + seg - back], 1, axis=0))
    for fwd in range(CONV_WIDTH - 1 - CONV_LEFT):
        xext_ref[CONV_LEFT + seg + fwd] = jnp.where(
            sub == SUBLANES - 1, xn8[fwd:fwd + 1, :],
            pltpu.roll(xext_ref[CONV_LEFT + fwd], SUBLANES - 1, axis=0))
    u3 = jnp.zeros((seg, SUBLANES, w), F32) + cb_ref[...]
    for k in range(CONV_WIDTH):
        u3 = u3 + cw_ref[k:k + 1, :] * xext_ref[k:k + seg]
    u = u3.reshape(t_t, w)

    ub = u.astype(BF16)
    nh = w // MXU_DIM
    pre_r = jnp.concatenate(
        [jnp.dot(ub[:, q * MXU_DIM:(q + 1) * MXU_DIM], wr_ref[q], preferred_element_type=F32)
         for q in range(nh)], axis=1) + br_ref[...]
    pre_i = jnp.concatenate(
        [jnp.dot(ub[:, q * MXU_DIM:(q + 1) * MXU_DIM], wi_ref[q], preferred_element_type=F32)
         for q in range(nh)], axis=1) + bi_ref[...]
    tr = jnp.tanh(pre_r)
    i = 0.5 + 0.5 * jnp.tanh(pre_i)
    nl = -lam_ref[...]
    en = jnp.exp(-jnp.abs(nl))
    w1 = 1.0 + en
    log1p_en = jnp.where(w1 == 1.0, en, en * jnp.log(w1) / jnp.where(w1 == 1.0, 1.0, w1 - 1.0))
    half_c = (-0.5 * LRU_C) * (jnp.maximum(nl, 0.0) + log1p_en)
    log_a = half_c + half_c * tr
    a = jnp.exp(log_a)
    v = -jnp.tanh(log_a) * (1.0 + a * a)
    gx = jnp.where(v > 0.0, v * lax.rsqrt(v), 0.0) * (i * u)
    a_ref[...] = a.reshape(seg, SUBLANES, w)
    b_ref[...] = gx.reshape(seg, SUBLANES, w)

    @pl.when(jnp.logical_and(p == 1, ti * t_t < pad))
    def _():
        j3 = lax.broadcasted_iota(jnp.int32, (seg, SUBLANES, 1), 0)
        s3 = lax.broadcasted_iota(jnp.int32, (seg, SUBLANES, 1), 1)
        rowidx = ti * t_t + s3 * seg + j3
        b_ref[...] = jnp.where(rowidx >= pad, b_ref[...], 0.0)

    @pl.when(p == 0)
    def _():
        hb_ref[ti] = _scan_tile(a_ref, b_ref, h_ref, acc_ref, carry_ref, seg, False)

    @pl.when(p == 1)
    def _():
        h = _scan_tile(a_ref, b_ref, h_ref, acc_ref, carry_ref, seg, True) + hb_ref[ti]
        _interleave_in(gate_ref, r0, slab_ref, b_ref, seg, 0)
        y = h * jax.nn.gelu(b_ref[...], approximate=True)
        ms = jnp.mean(y * y, axis=-1, keepdims=True)
        h_ref[...] = y * lax.rsqrt(ms + EPS) * og_ref[...]
        _interleave_out(h_ref, slab_ref, y_ref, r0, seg)


def _retention_kernel(q_ref, k_ref, v_ref, g_ref, dmat_ref, dvec_ref, cd_ref, og_ref, y_ref,
                      sb_ref, s_ref, *, n_t, tc, hd):
    p = pl.program_id(1)
    j = pl.program_id(2)
    ti = jnp.where(p == 0, n_t - 1 - j, j)
    nheads = q_ref.shape[1] // hd

    @pl.when(j == 0)
    def _():
        s_ref[...] = jnp.zeros_like(s_ref)

    def tn_dot(x, y):
        return lax.dot_general(x, y, (((0,), (0,)), ((), ())), preferred_element_type=F32)

    def nt_dot(x, y):
        return lax.dot_general(x, y, (((1,), (1,)), ((), ())), preferred_element_type=F32)

    @pl.when(p == 0)
    def _():
        for c in reversed(range(tc)):
            cg = ti * tc + c
            for h in range(nheads):
                rs = slice(c * CHUNK, (c + 1) * CHUNK)
                cs = slice(h * hd, (h + 1) * hd)
                kk = k_ref[rs, cs].astype(F32)
                vv = v_ref[rs, cs].astype(BF16)
                sb_ref[cg, h] = s_ref[h].astype(BF16)
                s_ref[h] = s_ref[h] * cd_ref[h] + tn_dot((kk * dvec_ref[3, h]).astype(BF16), vv)

    @pl.when(p == 1)
    def _():
        for c in range(tc):
            cg = ti * tc + c
            for h in range(nheads):
                rs = slice(c * CHUNK, (c + 1) * CHUNK)
                cs = slice(h * hd, (h + 1) * hd)
                qq = q_ref[rs, cs].astype(F32)
                kk = k_ref[rs, cs].astype(F32)
                vv = v_ref[rs, cs].astype(BF16)
                sc = nt_dot(qq.astype(BF16), kk.astype(BF16)) * dmat_ref[h]
                o = jnp.dot(sc.astype(BF16), vv, preferred_element_type=F32)
                o = o + jnp.dot((qq * dvec_ref[0, h]).astype(BF16), s_ref[h].astype(BF16),
                                preferred_element_type=F32)
                o = o + jnp.dot((qq * dvec_ref[1, h]).astype(BF16), sb_ref[cg, h],
                                preferred_element_type=F32)
                s_ref[h] = s_ref[h] * cd_ref[h] + tn_dot((kk * dvec_ref[2, h]).astype(BF16), vv)
                ms = jnp.mean(o * o, axis=-1, keepdims=True)
                o = o * lax.rsqrt(ms + EPS) * og_ref[:, cs]
                y_ref[rs, cs] = (o * jax.nn.silu(g_ref[rs, cs].astype(F32))).astype(y_ref.dtype)


def _split3(x):
    hi = x.astype(BF16)
    r1 = x - hi.astype(F32)
    mid = r1.astype(BF16)
    lo = (r1 - mid.astype(F32)).astype(BF16)
    return hi, mid, lo


def _outproj_kernel(yl_ref, yr_ref, head_ref, *refs, tc, ne):
    x_refs = refs[0:tc]
    w_ref, g_ref, wr2_ref, hs2_ref, h2e_ref, lg_ref, xt_ref = refs[tc:]
    _gather_row_tile(head_ref, x_refs, xt_ref)
    wl = yl_ref.shape[1]
    d = xt_ref.shape[1]
    mix = jnp.dot(yl_ref[...].astype(BF16), w_ref[0:wl, :], preferred_element_type=F32)
    mix = mix + jnp.dot(yr_ref[...].astype(BF16), w_ref[wl:, :], preferred_element_type=F32)
    hs2 = xt_ref[...] + mix
    hs2_ref[...] = hs2
    ms = jnp.mean(hs2 * hs2, axis=-1, keepdims=True)
    h2 = hs2 * lax.rsqrt(ms + EPS) * g_ref[...]
    hh = h2.astype(BF16)
    h2e_ref[:, 0:d] = hh
    hm = (h2 - hh.astype(F32)).astype(BF16)
    w2 = wr2_ref[...]
    d_hi = jnp.dot(hh, w2, preferred_element_type=F32)
    d_mid = jnp.dot(hm, w2, preferred_element_type=F32)
    lr = d_hi[:, 0:LANES] + (d_hi[:, LANES:] + d_mid[:, 0:LANES])
    lg_ref[...] = lr.T[0:ne, :]

    lane = lax.broadcasted_iota(jnp.int32, lr.shape, 1)
    live = lane < ne
    mx = jnp.max(jnp.where(live, lr, -jnp.inf), axis=1, keepdims=True)
    ex = jnp.where(live, jnp.exp(lr - mx), 0.0)
    aff = ex / jnp.sum(ex, axis=1, keepdims=True)
    a_hi = aff.astype(BF16).astype(F32)
    r1 = aff - a_hi
    a_mid = r1.astype(BF16).astype(F32)
    a_lo = (r1 - a_mid).astype(BF16).astype(F32)
    ext = a_hi + pltpu.roll(a_mid, ne, axis=1) + pltpu.roll(a_lo, 2 * ne, axis=1)
    h2e_ref[:, d:] = ext.astype(BF16)


def _route_kernel(lg_ref, slot_ref, off_ref, *, pad, tp, cap):
    ne, nc, _ = lg_ref.shape
    lg = lg_ref[...]
    m = jnp.max(lg, axis=0, keepdims=True)
    ex = jnp.exp(lg - m)
    aff = ex / jnp.sum(ex, axis=0, keepdims=True)
    tpos = (lax.broadcasted_iota(jnp.int32, (nc, LANES), 0) * LANES
            + lax.broadcasted_iota(jnp.int32, (nc, LANES), 1))
    valid = jnp.logical_and(tpos >= pad, tpos < tp)[None]
    affm = jnp.where(valid, aff, -1.0)

    def count(mask):
        c = jnp.sum(jnp.where(mask, 1.0, 0.0), axis=1, keepdims=True)
        return jnp.sum(c, axis=2, keepdims=True)

    def as_float(bits):
        return lax.bitcast_convert_type(bits, F32)

    def bs_body(_, lohi):
        lo, hi = lohi
        mid = lo + lax.shift_right_logical(hi - lo, 1)
        ok = count(affm >= as_float(mid)) >= float(cap)
        return jnp.where(ok, mid, lo), jnp.where(ok, hi, mid)

    lo0 = jnp.zeros((ne, 1, 1), jnp.int32)
    hi0 = jnp.full((ne, 1, 1), 0x3F800001, jnp.int32)
    thr_bits, _ = lax.fori_loop(0, 31, bs_body, (lo0, hi0))
    thr = as_float(thr_bits)

    gt = affm > thr
    eq = affm == thr
    need = float(cap) - count(gt)

    ci = lax.broadcasted_iota(jnp.int32, (LANES, LANES), 0)
    cj = lax.broadcasted_iota(jnp.int32, (LANES, LANES), 1)
    tri = (ci <= cj).astype(BF16)
    ri = lax.broadcasted_iota(jnp.int32, (nc, nc), 0)
    rj = lax.broadcasted_iota(jnp.int32, (nc, nc), 1)
    ltri = (rj < ri).astype(BF16)

    def prefix(mask2d):
        within = jnp.dot(mask2d.astype(BF16), tri, preferred_element_type=F32)
        tot = jnp.broadcast_to(within[:, LANES - 1:LANES], (nc, LANES)).astype(BF16)
        off = jnp.dot(ltri, tot, preferred_element_type=F32)
        return within + off, off

    for e in range(ne):
        tie_incl, _ = prefix(eq[e])
        tie_excl = tie_incl - eq[e].astype(F32)
        sel = jnp.logical_or(gt[e], jnp.logical_and(eq[e], tie_excl < need[e]))
        cum, off = prefix(sel)
        slot_ref[e] = jnp.where(sel, cum.astype(jnp.int32) - 1, -1)
        off_ref[e] = off.astype(jnp.int32)


def _dispatch_kernel(start_ref, h2e_ref, slot_ref, xs_ref, *, tcd, rw, s_rows, nc1, ne):
    b = pl.program_id(0)
    g = pl.program_id(1)
    i = pl.program_id(2)
    ge = slot_ref.shape[0]
    dext = h2e_ref.shape[1]

    @pl.when(i == 0)
    def _():
        xs_ref[...] = jnp.zeros_like(xs_ref)

    rw1, rw2 = rw
    for c in range(0, tcd, 2):
        gc = min(2, tcd - c)
        rw = rw2 if gc == 2 else rw1
        r_iota = lax.broadcasted_iota(jnp.int32, (rw, 1), 0)
        base = (b * nc1 + i * tcd + c) * ne + g * ge
        s1 = [start_ref[base + gc * ne + e] for e in range(ge)]
        lo = [(start_ref[base + e] // BF16_ROWS) * BF16_ROWS for e in range(ge)]
        npass = functools.reduce(jnp.maximum, [(s1[e] - lo[e] + rw - 1) // rw for e in range(ge)])
        rows = slice(c * CHUNK, (c + gc) * CHUNK)

        def one_pass(q, lo=lo, rows=rows, rw=rw, r_iota=r_iota):
            ws = [pl.multiple_of(jnp.minimum(lo[e] + q * rw, s_rows - rw), BF16_ROWS) for e in range(ge)]
            hits = []
            for e in range(ge):
                slot_row = slot_ref[e:e + 1, rows]
                hit = jnp.logical_and(slot_row == (ws[e] + r_iota), slot_row >= lo[e] + q * rw)
                hits.append(hit.astype(BF16))
            pc = jnp.concatenate(hits, axis=0)
            for n in range(0, dext, MXU_DIM):
                cs = slice(n, min(n + MXU_DIM, dext))
                res = jnp.dot(pc, h2e_ref[rows, cs], preferred_element_type=F32)
                for e in range(ge):
                    xs_ref[e, pl.ds(ws[e], rw), cs] += res[e * rw:(e + 1) * rw].astype(BF16)

        one_pass(0)

        def extra_pass(q, carry, one_pass=one_pass):
            one_pass(q)
            return carry

        lax.fori_loop(1, npass, extra_pass, 0)


def _ffn_kernel(xs_ref, wg_ref, wu_ref, wd_ref, y_ref, wgb_ref, wub_ref, wdb_ref, yacc_ref, *, ne, ff_blk):
    e = pl.program_id(0)

    @pl.when(pl.program_id(1) == 0)
    def _():
        wgb_ref[...] = wg_ref[...].astype(BF16)
        wub_ref[...] = wu_ref[...].astype(BF16)
        wdb_ref[...] = wd_ref[...].astype(BF16)

    wg_ref, wu_ref, wd_ref = wgb_ref, wub_ref, wdb_ref
    d = wg_ref.shape[0]
    xs = xs_ref[:, 0:d]
    ext = xs_ref[:, d:].astype(F32)
    lane = lax.broadcasted_iota(jnp.int32, ext.shape, 1)
    gate = jnp.sum(jnp.where(jnp.bitwise_and(lane, ne - 1) == e, ext, 0.0), axis=1, keepdims=True)
    ff = wg_ref.shape[1]
    for f in range(ff // ff_blk):
        fs = slice(f * ff_blk, (f + 1) * ff_blk)
        a = jnp.dot(xs, wg_ref[:, fs], preferred_element_type=F32)
        u = jnp.dot(xs, wu_ref[:, fs], preferred_element_type=F32)
        act = (jax.nn.silu(a) * u).astype(BF16)
        contrib = jnp.dot(act, wd_ref[fs, :], preferred_element_type=F32)
        if f == 0:
            yacc_ref[...] = contrib
        else:
            yacc_ref[...] += contrib
    y_ref[...] = (yacc_ref[...] * gate).astype(BF16)


def _combine_kernel(start_ref, *refs, c0, cpw, rc, s_rows, nc1):
    hs2_refs = refs[0:cpw]
    slot_refs = refs[cpw:2 * cpw]
    y_hbm, fg_ref, out_ref, ybuf_ref, yextra_ref, sem, sem_x = refs[2 * cpw:]
    b = pl.program_id(0)
    cc = pl.program_id(1)
    n_c = pl.num_programs(1)
    ne = slot_refs[0].shape[0]
    d = ybuf_ref.shape[3]
    n = b * n_c + cc
    par = n % 2
    r_iota = lax.broadcasted_iota(jnp.int32, (rc, 1), 0)

    def window_lo(bb, chunk):
        base = (bb * nc1 + chunk) * ne
        return [(start_ref[base + e] // BF16_ROWS) * BF16_ROWS for e in range(ne)]

    def window_start(lo_e, q):
        return pl.multiple_of(jnp.minimum(lo_e + q * rc, s_rows - rc), BF16_ROWS)

    def first_copies(bb, chunk, buf):
        lo_ = window_lo(bb, chunk)
        return [pltpu.make_async_copy(y_hbm.at[bb, e, pl.ds(window_start(lo_[e], 0), rc)],
                                      ybuf_ref.at[buf, e], sem.at[buf, e]) for e in range(ne)]

    @pl.when(n == 0)
    def _():
        for cp in first_copies(b, cc * cpw + c0, par):
            cp.start()

    @pl.when(n + 1 < pl.num_programs(0) * n_c)
    def _():
        n1 = n + 1
        for cp in first_copies(n1 // n_c, (n1 % n_c) * cpw + c0, 1 - par):
            cp.start()

    c = cc * cpw + c0
    base = (b * nc1 + c) * ne
    s1 = [start_ref[base + cpw * ne + e] for e in range(ne)]
    lo = window_lo(b, c)
    npass = functools.reduce(jnp.maximum, [(s1[e] - lo[e] + rc - 1) // rc for e in range(ne)])

    def onehots(q):
        hits = []
        for e in range(ne):
            slot_row = jnp.concatenate([slot_refs[k][e:e + 1, :] for k in range(cpw)], axis=1)
            hit = jnp.logical_and(slot_row == (window_start(lo[e], q) + r_iota), slot_row >= lo[e] + q * rc)
            hits.append(hit.astype(BF16))
        return jnp.concatenate(hits, axis=0)

    def tn_dot(pc, yy):
        return lax.dot_general(pc, yy, (((0,), (0,)), ((), ())), preferred_element_type=F32)

    pc0 = onehots(0)
    for cp in first_copies(b, c, par):
        cp.wait()
    acc0 = tn_dot(pc0, ybuf_ref[par].reshape(ne * rc, d))

    def extra_pass(q, acc):
        cps = [pltpu.make_async_copy(y_hbm.at[b, e, pl.ds(window_start(lo[e], q), rc)],
                                     yextra_ref.at[e], sem_x.at[e]) for e in range(ne)]
        for cp in cps:
            cp.start()
        pcq = onehots(q)
        for cp in cps:
            cp.wait()
        return acc + tn_dot(pcq, yextra_ref[...].reshape(ne * rc, d))

    acc = lax.fori_loop(1, npass, extra_pass, acc0)
    for k in range(cpw):
        hs3 = hs2_refs[k][...] + acc[k * CHUNK:(k + 1) * CHUNK, :]
        ms = jnp.mean(hs3 * hs3, axis=-1, keepdims=True)
        out_ref[k * CHUNK:(k + 1) * CHUNK, :] = hs3 * lax.rsqrt(ms + EPS) * fg_ref[...]


def _vmem(mb):
    return mb * 1024 * 1024


def kernel(x, meta_tokens, norm1_g, w_in, conv_w, conv_b, lru_w_r, lru_b_r, lru_w_i, lru_b_i,
           lru_lambda, lru_out_g, ret_out_g, w_out, norm2_g, w_router, w_gate, w_up, w_down, final_g):
    bsz, seq, d = x.shape
    depth = norm1_g.shape[0]
    assert depth == 1
    n_meta = meta_tokens.shape[0]
    t_len = seq + n_meta
    pad = (-t_len) % CHUNK
    tp = t_len + pad
    nc = tp // CHUNK
    assert (pad + n_meta) % CHUNK == 0
    c0 = (pad + n_meta) // CHUNK
    rows = bsz * tp
    w_lru = conv_w.shape[-1]
    w_ret = ret_out_g.shape[-1]
    in_cols = w_in.shape[-1]
    cb = w_lru
    assert w_ret == cb and in_cols == 6 * cb and cb % MXU_DIM == 0
    hd = w_ret // RET_HEADS
    ne = w_router.shape[-1]
    ff = w_gate.shape[-1]
    cap = EC_CAPACITY * t_len // ne

    tc = _largest_divisor_leq(nc, 5)
    tm = tc * CHUNK
    n_t = nc // tc
    n_tiles = rows // tm

    assert c0 == 1 and seq % CHUNK == 0
    head = jnp.concatenate([jnp.zeros((pad, d), F32), meta_tokens.astype(F32)], axis=0)
    x4 = x.astype(F32).reshape(bsz, seq // CHUNK, CHUNK, d)

    def x_piece_spec(k):
        return pl.BlockSpec((None, None, CHUNK, d),
                            lambda b, j: (b, jnp.maximum(j * tc + k - 1, 0), 0, 0))

    row_tile_specs = [pl.BlockSpec((CHUNK, d), lambda b, j: (0, 0))] + [x_piece_spec(k) for k in range(tc)]

    half = hd // 2
    freqs = ROPE_BASE ** (-jnp.arange(half, dtype=F32) / half)
    ang = (jnp.arange(tp) - pad).astype(F32)[:, None] * freqs[None, :]
    cos2 = jnp.concatenate([jnp.cos(ang), jnp.cos(ang)], axis=1)
    sin2 = jnp.concatenate([-jnp.sin(ang), jnp.sin(ang)], axis=1)

    w_in_b = w_in[0].astype(BF16)
    w_out_b = w_out[0].astype(BF16)

    z = pl.pallas_call(
        functools.partial(_inproj_kernel, tc=tc, cb=cb, hd=hd, q_scale=float(hd) ** -0.5),
        grid=(bsz, n_t),
        in_specs=row_tile_specs + [
            pl.BlockSpec((1, d), lambda b, j: (0, 0)),
            pl.BlockSpec((d, in_cols), lambda b, j: (0, 0)),
            pl.BlockSpec((tm, hd), lambda b, j: (j, 0)),
            pl.BlockSpec((tm, hd), lambda b, j: (j, 0)),
        ],
        out_specs=pl.BlockSpec((tm, in_cols), lambda b, j: (b * n_t + j, 0)),
        out_shape=jax.ShapeDtypeStruct((rows, in_cols), BF16),
        scratch_shapes=[pltpu.VMEM((tm, d), F32)],
        compiler_params=pltpu.CompilerParams(dimension_semantics=("arbitrary",) * 2,
                                             vmem_limit_bytes=_vmem(52)),
        name="inproj",
    )(head, *([x4] * tc), norm1_g[0][None], w_in_b, cos2, sin2)

    lb = w_lru // LRU_BLOCKS
    per_tile = MXU_DIM // lb
    eye = jnp.eye(per_tile, dtype=F32)

    def blockdiag(wb):
        wb = wb.reshape(2, w_lru // MXU_DIM, per_tile, lb, lb)
        return jnp.einsum('dhpij,pq->dhpiqj', wb, eye).reshape(2, w_lru // MXU_DIM, MXU_DIM, MXU_DIM).astype(BF16)

    wr_bd = blockdiag(0.5 * lru_w_r[0])
    wi_bd = blockdiag(0.5 * lru_w_i[0])
    t_l = max(t for t in range(32, min(tp, 512) + 1, 32) if tp % t == 0 and (t // 32) % 2 == 1)
    n_l = tp // t_l
    seg = t_l // SUBLANES
    assert seg >= CONV_WIDTH and cb % LANES == 0
    nsub = _largest_divisor_leq(n_l, 5)
    t_b = nsub * t_l
    n_lb = n_l // nsub
    hb8 = t_b // BF16_ROWS
    last8 = rows // BF16_ROWS - 1

    def ltile_of(b, p, j):
        return b * n_lb + jnp.where(p == 0, n_lb - 1 - j, j)

    y_lru = pl.pallas_call(
        functools.partial(_lru_kernel, n_t=n_l, t_t=t_l, pad=pad),
        grid=(bsz, 2, n_lb),
        in_specs=[
            pl.BlockSpec((t_b, cb), lambda b, p, j: (ltile_of(b, p, j), 0)),
            pl.BlockSpec((BF16_ROWS, cb), lambda b, p, j: (jnp.maximum(ltile_of(b, p, j) * hb8 - 1, 0), 0)),
            pl.BlockSpec((BF16_ROWS, cb), lambda b, p, j: (jnp.minimum((ltile_of(b, p, j) + 1) * hb8, last8), 0)),
            pl.BlockSpec((t_b, cb), lambda b, p, j: (b * n_lb + p * j, 1)),
            pl.BlockSpec((CONV_WIDTH, cb), lambda b, p, j: (0, 0)),
            pl.BlockSpec((1, cb), lambda b, p, j: (0, 0)),
            pl.BlockSpec((None, w_lru // MXU_DIM, MXU_DIM, MXU_DIM), lambda b, p, j: (1 - p, 0, 0, 0)),
            pl.BlockSpec((None, w_lru // MXU_DIM, MXU_DIM, MXU_DIM), lambda b, p, j: (1 - p, 0, 0, 0)),
            pl.BlockSpec((None, 1, cb), lambda b, p, j: (1 - p, 0, 0)),
            pl.BlockSpec((None, 1, cb), lambda b, p, j: (1 - p, 0, 0)),
            pl.BlockSpec((None, 1, cb), lambda b, p, j: (1 - p, 0, 0)),
            pl.BlockSpec((1, cb), lambda b, p, j: (0, 0)),
        ],
        out_specs=pl.BlockSpec((t_b, cb), lambda b, p, j: (b * n_lb + p * j, 0)),
        out_shape=jax.ShapeDtypeStruct((rows, cb), BF16),
        scratch_shapes=[
            pltpu.VMEM((n_l, seg, SUBLANES, cb), F32),
            pltpu.VMEM((cb // LANES, t_l, LANES), F32),
            pltpu.VMEM((seg + CONV_WIDTH - 1, SUBLANES, cb), F32),
            pltpu.VMEM((seg, SUBLANES, cb), F32),
            pltpu.VMEM((seg, SUBLANES, cb), F32),
            pltpu.VMEM((seg, SUBLANES, cb), F32),
            pltpu.VMEM((seg, SUBLANES, cb), F32),
            pltpu.VMEM((1, cb), F32),
        ],
        compiler_params=pltpu.CompilerParams(dimension_semantics=("arbitrary",) * 3,
                                             vmem_limit_bytes=_vmem(52)),
        name="lru",
    )(z, z, z, z, conv_w[0], conv_b[0][None], wr_bd, wi_bd, 0.5 * lru_b_r[0][:, None], 0.5 * lru_b_i[0][:, None],
      lru_lambda[0][:, None], lru_out_g[0][None])

    log_g = jnp.log(1.0 - jnp.exp2(-5.0 - jnp.arange(RET_HEADS, dtype=F32)))
    idx = jnp.arange(CHUNK, dtype=F32)
    dmat = jnp.exp(log_g[:, None, None] * jnp.abs(idx[:, None] - idx[None, :]))

    def posdec(expo):
        return jnp.broadcast_to(jnp.exp(log_g[:, None] * expo[None, :])[:, :, None], (RET_HEADS, CHUNK, hd))

    dvec = jnp.stack([posdec(idx + 1.0), posdec(CHUNK - idx), posdec(CHUNK - 1.0 - idx), posdec(idx)])
    cdm = jnp.broadcast_to(jnp.exp(log_g * CHUNK)[:, None, None], (RET_HEADS, hd, hd))

    tc_r = _largest_divisor_leq(nc, 13)
    tm_r = tc_r * CHUNK
    n_r = nc // tc_r

    def rtile_of(b, p, j):
        return b * n_r + jnp.where(p == 0, n_r - 1 - j, j)

    y_ret = pl.pallas_call(
        functools.partial(_retention_kernel, n_t=n_r, tc=tc_r, hd=hd),
        grid=(bsz, 2, n_r),
        in_specs=[
            pl.BlockSpec((tm_r, cb), lambda b, p, j: (b * n_r + p * j, 2)),
            pl.BlockSpec((tm_r, cb), lambda b, p, j: (rtile_of(b, p, j), 3)),
            pl.BlockSpec((tm_r, cb), lambda b, p, j: (rtile_of(b, p, j), 4)),
            pl.BlockSpec((tm_r, cb), lambda b, p, j: (b * n_r + p * j, 5)),
            pl.BlockSpec((RET_HEADS, CHUNK, CHUNK), lambda b, p, j: (0, 0, 0)),
            pl.BlockSpec((4, RET_HEADS, CHUNK, hd), lambda b, p, j: (0, 0, 0, 0)),
            pl.BlockSpec((RET_HEADS, hd, hd), lambda b, p, j: (0, 0, 0)),
            pl.BlockSpec((1, cb), lambda b, p, j: (0, 0)),
        ],
        out_specs=pl.BlockSpec((tm_r, cb), lambda b, p, j: (b * n_r + p * j, 0)),
        out_shape=jax.ShapeDtypeStruct((rows, cb), BF16),
        scratch_shapes=[
            pltpu.VMEM((nc, RET_HEADS, hd, hd), BF16),
            pltpu.VMEM((RET_HEADS, hd, hd), F32),
        ],
        compiler_params=pltpu.CompilerParams(dimension_semantics=("arbitrary",) * 3,
                                             vmem_limit_bytes=_vmem(52)),
        name="retention",
    )(z, z, z, z, dmat, dvec, cdm, ret_out_g[0][None])

    assert ne & (ne - 1) == 0 and 3 * ne <= LANES
    dext = d + LANES
    w_router_pad = jnp.pad(w_router[0].astype(F32), ((0, 0), (0, LANES - ne)))
    wr_hi = w_router_pad.astype(BF16)
    wr_mid = (w_router_pad - wr_hi.astype(F32)).astype(BF16)
    hs2, h2e, logits_t = pl.pallas_call(
        functools.partial(_outproj_kernel, tc=tc, ne=ne),
        grid=(bsz, n_t),
        in_specs=[
            pl.BlockSpec((tm, cb), lambda b, j: (b * n_t + j, 0)),
            pl.BlockSpec((tm, cb), lambda b, j: (b * n_t + j, 0)),
        ] + row_tile_specs + [
            pl.BlockSpec((d, d), lambda b, j: (0, 0)),
            pl.BlockSpec((1, d), lambda b, j: (0, 0)),
            pl.BlockSpec((d, 2 * LANES), lambda b, j: (0, 0)),
        ],
        out_specs=[
            pl.BlockSpec((tm, d), lambda b, j: (b * n_t + j, 0)),
            pl.BlockSpec((tm, dext), lambda b, j: (b * n_t + j, 0)),
            pl.BlockSpec((None, ne, tm), lambda b, j: (b, 0, j)),
        ],
        out_shape=[
            jax.ShapeDtypeStruct((rows, d), F32),
            jax.ShapeDtypeStruct((rows, dext), BF16),
            jax.ShapeDtypeStruct((bsz, ne, tp), F32),
        ],
        scratch_shapes=[pltpu.VMEM((tm, d), F32)],
        compiler_params=pltpu.CompilerParams(dimension_semantics=("arbitrary",) * 2,
                                             vmem_limit_bytes=_vmem(52)),
        name="outproj",
    )(y_lru, y_ret, head, *([x4] * tc), w_out_b, norm2_g[0][None],
      jnp.concatenate([wr_hi, wr_mid], axis=1))

    ncp = _round_up(nc, BF16_ROWS)
    logits4 = jnp.pad(logits_t.reshape(bsz, ne, nc, LANES), ((0, 0), (0, 0), (0, ncp - nc), (0, 0)))
    slot4, off4 = pl.pallas_call(
        functools.partial(_route_kernel, pad=pad, tp=tp, cap=cap),
        grid=(bsz,),
        in_specs=[pl.BlockSpec((None, ne, ncp, LANES), lambda b: (b, 0, 0, 0))],
        out_specs=[pl.BlockSpec((None, ne, ncp, LANES), lambda b: (b, 0, 0, 0))] * 2,
        out_shape=[
            jax.ShapeDtypeStruct((bsz, ne, ncp, LANES), jnp.int32),
            jax.ShapeDtypeStruct((bsz, ne, ncp, LANES), jnp.int32),
        ],
        compiler_params=pltpu.CompilerParams(dimension_semantics=("arbitrary",),
                                             vmem_limit_bytes=_vmem(52)),
        name="route",
    )(logits4)

    start = jnp.concatenate([jnp.transpose(off4[:, :, :nc, 0], (0, 2, 1)),
                             jnp.full((bsz, 1, ne), cap, jnp.int32)], axis=1).reshape(-1)
    nc1 = nc + 1
    slot_rows = slot4[:, :, :nc].reshape(bsz, ne, tp)

    s_rows = _round_up(cap, BF16_ROWS)
    tcd = _largest_divisor_leq(nc, 13)
    w_d = tcd * CHUNK
    n_d = nc // tcd
    rw = (DISPATCH_WINDOW, DISPATCH_WINDOW_PAIR)
    ge = min(ne, SUBLANES)
    assert all(s_rows >= r and (s_rows - r) % BF16_ROWS == 0 for r in rw) and ne % ge == 0
    xs = pl.pallas_call(
        functools.partial(_dispatch_kernel, tcd=tcd, rw=rw, s_rows=s_rows, nc1=nc1, ne=ne),
        grid_spec=pltpu.PrefetchScalarGridSpec(
            num_scalar_prefetch=1,
            grid=(bsz, ne // ge, n_d),
            in_specs=[
                pl.BlockSpec((w_d, dext), lambda b, g, i, st: (b * n_d + i, 0)),
                pl.BlockSpec((None, ge, w_d), lambda b, g, i, st: (b, g, i)),
            ],
            out_specs=pl.BlockSpec((None, ge, s_rows, dext), lambda b, g, i, st: (b, g, 0, 0)),
        ),
        out_shape=jax.ShapeDtypeStruct((bsz, ne, s_rows, dext), BF16),
        compiler_params=pltpu.CompilerParams(dimension_semantics=("arbitrary",) * 3,
                                             vmem_limit_bytes=_vmem(52)),
        name="dispatch",
    )(start, h2e, slot_rows)

    ff_blk = 512 if ff % 512 == 0 else ff
    y_e = pl.pallas_call(
        functools.partial(_ffn_kernel, ne=ne, ff_blk=ff_blk),
        grid=(ne, bsz),
        in_specs=[
            pl.BlockSpec((None, None, s_rows, dext), lambda e, b: (b, e, 0, 0)),
            pl.BlockSpec((None, d, ff), lambda e, b: (e, 0, 0)),
            pl.BlockSpec((None, d, ff), lambda e, b: (e, 0, 0)),
            pl.BlockSpec((None, ff, d), lambda e, b: (e, 0, 0)),
        ],
        out_specs=pl.BlockSpec((None, None, s_rows, d), lambda e, b: (b, e, 0, 0)),
        out_shape=jax.ShapeDtypeStruct((bsz, ne, s_rows, d), BF16),
        scratch_shapes=[
            pltpu.VMEM((d, ff), BF16),
            pltpu.VMEM((d, ff), BF16),
            pltpu.VMEM((ff, d), BF16),
            pltpu.VMEM((s_rows, d), F32),
        ],
        compiler_params=pltpu.CompilerParams(dimension_semantics=("arbitrary",) * 2,
                                             vmem_limit_bytes=_vmem(56)),
        name="ffn",
    )(xs, w_gate[0], w_up[0], w_down[0])

    rc = COMBINE_WINDOW
    cpw = COMBINE_CHUNKS if (nc - c0) % COMBINE_CHUNKS == 0 else 1
    assert (s_rows - rc) % BF16_ROWS == 0
    slot_t = jnp.transpose(slot4[:, :, :nc], (0, 2, 1, 3))
    out = pl.pallas_call(
        functools.partial(_combine_kernel, c0=c0, cpw=cpw, rc=rc, s_rows=s_rows, nc1=nc1),
        grid_spec=pltpu.PrefetchScalarGridSpec(
            num_scalar_prefetch=1,
            grid=(bsz, (nc - c0) // cpw),
            in_specs=(
                [pl.BlockSpec((CHUNK, d), functools.partial(
                    lambda b, c, st, k: (b * nc + c * cpw + c0 + k, 0), k=k)) for k in range(cpw)]
                + [pl.BlockSpec((None, None, ne, LANES), functools.partial(
                    lambda b, c, st, k: (b, c * cpw + c0 + k, 0, 0), k=k)) for k in range(cpw)]
                + [pl.BlockSpec(memory_space=pl.ANY),
                   pl.BlockSpec((1, d), lambda b, c, st: (0, 0))]),
            out_specs=pl.BlockSpec((None, cpw * CHUNK, d), lambda b, c, st: (b, c, 0)),
            scratch_shapes=[
                pltpu.VMEM((2, ne, rc, d), BF16),
                pltpu.VMEM((ne, rc, d), BF16),
                pltpu.SemaphoreType.DMA((2, ne)),
                pltpu.SemaphoreType.DMA((ne,)),
            ],
        ),
        out_shape=jax.ShapeDtypeStruct((bsz, seq, d), x.dtype),
        compiler_params=pltpu.CompilerParams(dimension_semantics=("arbitrary",) * 2,
                                             vmem_limit_bytes=_vmem(32)),
        name="combine",
    )(start, *([hs2] * cpw), *([slot_t] * cpw), y_e, final_g[None])
    return out
```

```python
import functools

import jax
import jax.numpy as jnp
from jax import lax
from jax.experimental import pallas as pl
from jax.experimental.pallas import tpu as pltpu

N_META = 16
LRU_BLOCKS = 8
LRU_C = 8.0
CONV_WIDTH = 4
CONV_LEFT = 2
RET_HEADS = 4
CHUNK = 128
ROPE_BASE = 10000.0
N_EXPERTS = 16
EC_CAPACITY = 2
EPS = 1e-6

LANES = 128
SUBLANES = 8
BF16_ROWS = 16
MXU_DIM = 256

DISPATCH_WINDOW = 3 * BF16_ROWS
DISPATCH_WINDOW_PAIR = 4 * BF16_ROWS
COMBINE_WINDOW = 4 * BF16_ROWS
COMBINE_CHUNKS = 2

F32 = jnp.float32
BF16 = jnp.bfloat16


def _round_up(x, m):
    return (x + m - 1) // m * m


def _largest_divisor_leq(n, k):
    for d in range(min(n, k), 0, -1):
        if n % d == 0:
            return d
    return 1


def _gather_row_tile(head_ref, x_refs, dst_ref):
    first = pl.program_id(1) == 0
    for k, xr in enumerate(x_refs):
        piece = xr[...]
        if k == 0:
            piece = jnp.where(first, head_ref[...], piece)
        dst_ref[k * CHUNK:(k + 1) * CHUNK, :] = piece


def _inproj_kernel(head_ref, *refs, tc, cb, hd, q_scale):
    x_refs = refs[0:tc]
    g_ref, w_ref, cos_ref, sin_ref, z_ref, xt_ref = refs[tc:]
    _gather_row_tile(head_ref, x_refs, xt_ref)
    x = xt_ref[...]
    ms = jnp.mean(x * x, axis=-1, keepdims=True)
    h = (x * lax.rsqrt(ms + EPS) * g_ref[...]).astype(BF16)
    cos = cos_ref[...]
    sin = sin_ref[...]
    for j in range(w_ref.shape[1] // cb):
        zj = jnp.dot(h, w_ref[:, j * cb:(j + 1) * cb], preferred_element_type=F32)
        if j in (2, 3):
            for hh in range(cb // hd):
                t = zj[:, hh * hd:(hh + 1) * hd]
                r = pltpu.roll(t, hd // 2, axis=1)
                t = t * cos + r * sin
                if j == 2:
                    t = t * q_scale
                z_ref[:, j * cb + hh * hd:j * cb + (hh + 1) * hd] = t.astype(z_ref.dtype)
        else:
            z_ref[:, j * cb:(j + 1) * cb] = zj.astype(z_ref.dtype)


def _interleave_in(src_ref, r0, slab_ref, dst_ref, seg, off):
    nsl, t_t = slab_ref.shape[0], slab_ref.shape[1]
    for c in range(nsl):
        slab_ref[c] = src_ref[pl.ds(r0, t_t), c * LANES:(c + 1) * LANES].astype(F32)
    for j in range(seg):
        dst_ref[off + j] = jnp.concatenate(
            [slab_ref[c, pl.ds(j, SUBLANES, stride=seg), :] for c in range(nsl)], axis=1)


def _interleave_out(src_ref, slab_ref, dst_ref, r0, seg):
    nsl, t_t = slab_ref.shape[0], slab_ref.shape[1]
    for j in range(seg):
        row = src_ref[j]
        for c in range(nsl):
            slab_ref[c, pl.ds(j, SUBLANES, stride=seg), :] = row[:, c * LANES:(c + 1) * LANES]
    for c in range(nsl):
        dst_ref[pl.ds(r0, t_t), c * LANES:(c + 1) * LANES] = slab_ref[c].astype(dst_ref.dtype)


def _scan_tile(a_ref, b_ref, h_ref, acc_ref, carry_ref, seg, forward):
    w = a_ref.shape[2]

    unroll = 4 if seg % 4 == 0 else 1

    def body(jb, carry):
        h, acc = carry
        for jo in range(unroll):
            jj = jb * unroll + jo
            j = jj if forward else seg - 1 - jj
            a = a_ref[j]
            h = a * h + b_ref[j]
            acc = a * acc
            h_ref[j] = h
            acc_ref[j] = acc
        return h, acc

    h_end, a_end = lax.fori_loop(0, seg // unroll, body,
                                 (jnp.zeros((SUBLANES, w), F32), jnp.ones((SUBLANES, w), F32)))
    c = carry_ref[...]
    rows = [None] * SUBLANES
    order = range(SUBLANES) if forward else range(SUBLANES - 1, -1, -1)
    for s in order:
        rows[s] = c
        c = a_end[s:s + 1, :] * c + h_end[s:s + 1, :]
    carry_ref[...] = c
    cvec = jnp.concatenate(rows, axis=0)
    return h_ref[...] + acc_ref[...] * cvec[None]


def _lru_kernel(xm_ref, xp_ref, xn_ref, gate_ref, cw_ref, cb_ref, wr_ref, wi_ref, br_ref, bi_ref,
                lam_ref, og_ref, y_ref, hb_ref, slab_ref, xext_ref, a_ref, b_ref, h_ref, acc_ref, carry_ref,
                *, n_t, t_t, pad):
    p = pl.program_id(1)

    @pl.when(pl.program_id(2) == 0)
    def _():
        carry_ref[...] = jnp.zeros_like(carry_ref)

    nsub = xm_ref.shape[0] // t_t

    def sub_tile(k, carry):
        kk = jnp.where(p == 0, nsub - 1 - k, k)
        blk = jnp.where(p == 0, pl.num_programs(2) - 1 - pl.program_id(2), pl.program_id(2))
        _lru_tile(kk, blk * nsub + kk, xm_ref, xp_ref, xn_ref, gate_ref, cw_ref, cb_ref, wr_ref, wi_ref,
                  br_ref, bi_ref, lam_ref, og_ref, y_ref, hb_ref, slab_ref, xext_ref, a_ref, b_ref, h_ref,
                  acc_ref, carry_ref, n_t=n_t, t_t=t_t, pad=pad)
        return carry

    lax.fori_loop(0, nsub, sub_tile, 0)


def _lru_tile(kk, ti, xm_ref, xp_ref, xn_ref, gate_ref, cw_ref, cb_ref, wr_ref, wi_ref, br_ref, bi_ref,
              lam_ref, og_ref, y_ref, hb_ref, slab_ref, xext_ref, a_ref, b_ref, h_ref, acc_ref, carry_ref,
              *, n_t, t_t, pad):
    p = pl.program_id(1)
    w = xm_ref.shape[1]
    seg = t_t // SUBLANES
    nsub = xm_ref.shape[0] // t_t
    nhalo = xp_ref.shape[0]
    r0 = pl.multiple_of(kk * t_t, BF16_ROWS)

    _interleave_in(xm_ref, r0, slab_ref, xext_ref, seg, CONV_LEFT)
    sub = lax.broadcasted_iota(jnp.int32, (SUBLANES, w), 0)
    before = xm_ref[pl.ds(pl.multiple_of(jnp.maximum(r0 - nhalo, 0), BF16_ROWS), nhalo), :]
    after = xm_ref[pl.ds(pl.multiple_of(jnp.minimum(r0 + t_t, (nsub - 1) * t_t), BF16_ROWS), nhalo), :]
    xp8 = jnp.where(kk == 0, xp_ref[...], before).astype(F32)
    xn8 = jnp.where(kk == nsub - 1, xn_ref[...], after).astype(F32)
    xp8 = jnp.where(ti == 0, 0.0, xp8)
    xn8 = jnp.where(ti == n_t - 1, 0.0, xn8)
    for back in range(1, CONV_LEFT + 1):
        xext_ref[CONV_LEFT - back] = jnp.where(
            sub == 0, xp8[nhalo - back:nhalo - back + 1, :],
            pltpu.roll(xext_ref[CONV_LEFT + seg - back], 1, axis=0))
    for fwd in range(CONV_WIDTH - 1 - CONV_LEFT):
        xext_ref[CONV_LEFT + seg + fwd] = jnp.where(
            sub == SUBLANES - 1, xn8[fwd:fwd + 1, :],
            pltpu.roll(xext_ref[CONV_LEFT + fwd], SUBLANES - 1, axis=0))
    u3 = jnp.zeros((seg, SUBLANES, w), F32) + cb_ref[...]
    for k in range(CONV_WIDTH):
        u3 = u3 + cw_ref[k:k + 1, :] * xext_ref[k:k + seg]
    u = u3.reshape(t_t, w)

    ub = u.astype(BF16)
    nh = w // MXU_DIM
    pre_r = jnp.concatenate(
        [jnp.dot(ub[:, q * MXU_DIM:(q + 1) * MXU_DIM], wr_ref[q], preferred_element_type=F32)
         for q in range(nh)], axis=1) + br_ref[...]
    pre_i = jnp.concatenate(
        [jnp.dot(ub[:, q * MXU_DIM:(q + 1) * MXU_DIM], wi_ref[q], preferred_element_type=F32)
         for q in range(nh)], axis=1) + bi_ref[...]
    tr = jnp.tanh(pre_r)
    i = 0.5 + 0.5 * jnp.tanh(pre_i)
    nl = -lam_ref[...]
    en = jnp.exp(-jnp.abs(nl))
    w1 = 1.0 + en
    log1p_en = jnp.where(w1 == 1.0, en, en * jnp.log(w1) / jnp.where(w1 == 1.0, 1.0, w1 - 1.0))
    half_c = (-0.5 * LRU_C) * (jnp.maximum(nl, 0.0) + log1p_en)
    log_a = half_c + half_c * tr
    a = jnp.exp(log_a)
    v = -jnp.tanh(log_a) * (1.0 + a * a)
    gx = jnp.where(v > 0.0, v * lax.rsqrt(v), 0.0) * (i * u)
    a_ref[...] = a.reshape(seg, SUBLANES, w)
    b_ref[...] = gx.reshape(seg, SUBLANES, w)

    @pl.when(jnp.logical_and(p == 1, ti * t_t < pad))
    def _():
        j3 = lax.broadcasted_iota(jnp.int32, (seg, SUBLANES, 1), 0)
        s3 = lax.broadcasted_iota(jnp.int32, (seg, SUBLANES, 1), 1)
        rowidx = ti * t_t + s3 * seg + j3
        b_ref[...] = jnp.where(rowidx >= pad, b_ref[...], 0.0)

    @pl.when(p == 0)
    def _():
        hb_ref[ti] = _scan_tile(a_ref, b_ref, h_ref, acc_ref, carry_ref, seg, False)

    @pl.when(p == 1)
    def _():
        h = _scan_tile(a_ref, b_ref, h_ref, acc_ref, carry_ref, seg, True) + hb_ref[ti]
        _interleave_in(gate_ref, r0, slab_ref, b_ref, seg, 0)
        y = h * jax.nn.gelu(b_ref[...], approximate=True)
        ms = jnp.mean(y * y, axis=-1, keepdims=True)
        h_ref[...] = y * lax.rsqrt(ms + EPS) * og_ref[...]
        _interleave_out(h_ref, slab_ref, y_ref, r0, seg)


def _retention_kernel(q_ref, k_ref, v_ref, g_ref, dmat_ref, dvec_ref, cd_ref, og_ref, y_ref,
                      sb_ref, s_ref, *, n_t, tc, hd):
    p = pl.program_id(1)
    j = pl.program_id(2)
    ti = jnp.where(p == 0, n_t - 1 - j, j)
    nheads = q_ref.shape[1] // hd

    @pl.when(j == 0)
    def _():
        s_ref[...] = jnp.zeros_like(s_ref)

    def tn_dot(x, y):
        return lax.dot_general(x, y, (((0,), (0,)), ((), ())), preferred_element_type=F32)

    def nt_dot(x, y):
        return lax.dot_general(x, y, (((1,), (1,)), ((), ())), preferred_element_type=F32)

    @pl.when(p == 0)
    def _():
        for c in reversed(range(tc)):
            cg = ti * tc + c
            for h in range(nheads):
                rs = slice(c * CHUNK, (c + 1) * CHUNK)
                cs = slice(h * hd, (h + 1) * hd)
                kk = k_ref[rs, cs].astype(F32)
                vv = v_ref[rs, cs].astype(BF16)
                sb_ref[cg, h] = s_ref[h].astype(BF16)
                s_ref[h] = s_ref[h] * cd_ref[h] + tn_dot((kk * dvec_ref[3, h]).astype(BF16), vv)

    @pl.when(p == 1)
    def _():
        for c in range(tc):
            cg = ti * tc + c
            for h in range(nheads):
                rs = slice(c * CHUNK, (c + 1) * CHUNK)
                cs = slice(h * hd, (h + 1) * hd)
                qq = q_ref[rs, cs].astype(F32)
                kk = k_ref[rs, cs].astype(F32)
                vv = v_ref[rs, cs].astype(BF16)
                sc = nt_dot(qq.astype(BF16), kk.astype(BF16)) * dmat_ref[h]
                o = jnp.dot(sc.astype(BF16), vv, preferred_element_type=F32)
                o = o + jnp.dot((qq * dvec_ref[0, h]).astype(BF16), s_ref[h].astype(BF16),
                                preferred_element_type=F32)
                o = o + jnp.dot((qq * dvec_ref[1, h]).astype(BF16), sb_ref[cg, h],
                                preferred_element_type=F32)
                s_ref[h] = s_ref[h] * cd_ref[h] + tn_dot((kk * dvec_ref[2, h]).astype(BF16), vv)
                ms = jnp.mean(o * o, axis=-1, keepdims=True)
                o = o * lax.rsqrt(ms + EPS) * og_ref[:, cs]
                y_ref[rs, cs] = (o * jax.nn.silu(g_ref[rs, cs].astype(F32))).astype(y_ref.dtype)


def _split3(x):
    hi = x.astype(BF16)
    r1 = x - hi.astype(F32)
    mid = r1.astype(BF16)
    lo = (r1 - mid.astype(F32)).astype(BF16)
    return hi, mid, lo


def _outproj_kernel(yl_ref, yr_ref, head_ref, *refs, tc, ne):
    x_refs = refs[0:tc]
    w_ref, g_ref, wr2_ref, hs2_ref, h2e_ref, lg_ref, xt_ref = refs[tc:]
    _gather_row_tile(head_ref, x_refs, xt_ref)
    wl = yl_ref.shape[1]
    d = xt_ref.shape[1]
    tm = xt_ref.shape[0]
    cut = ((tc + 1) // 2) * CHUNK
    for rs in ([slice(0, cut), slice(cut, tm)] if cut < tm else [slice(0, tm)]):
        mix = jnp.dot(yl_ref[rs, :].astype(BF16), w_ref[0:wl, :], preferred_element_type=F32)
        mix = mix + jnp.dot(yr_ref[rs, :].astype(BF16), w_ref[wl:, :], preferred_element_type=F32)
        hs2 = xt_ref[rs, :] + mix
        hs2_ref[rs, :] = hs2.astype(hs2_ref.dtype)
        ms = jnp.mean(hs2 * hs2, axis=-1, keepdims=True)
        h2 = hs2 * lax.rsqrt(ms + EPS) * g_ref[...]
        hh = h2.astype(BF16)
        h2e_ref[rs, 0:d] = hh
        hm = (h2 - hh.astype(F32)).astype(BF16)
        w2 = wr2_ref[...]
        d_hi = jnp.dot(hh, w2, preferred_element_type=F32)
        d_mid = jnp.dot(hm, w2, preferred_element_type=F32)
        lr = d_hi[:, 0:LANES] + (d_hi[:, LANES:] + d_mid[:, 0:LANES])
        lg_ref[:, rs] = lr.T[0:ne, :]

        lane = lax.broadcasted_iota(jnp.int32, lr.shape, 1)
        live = lane < ne
        mx = jnp.max(jnp.where(live, lr, -jnp.inf), axis=1, keepdims=True)
        ex = jnp.where(live, jnp.exp(lr - mx), 0.0)
        aff = ex / jnp.sum(ex, axis=1, keepdims=True)
        a_hi = aff.astype(BF16).astype(F32)
        r1 = aff - a_hi
        a_mid = r1.astype(BF16).astype(F32)
        a_lo = (r1 - a_mid).astype(BF16).astype(F32)
        ext = a_hi + pltpu.roll(a_mid, ne, axis=1) + pltpu.roll(a_lo, 2 * ne, axis=1)
        h2e_ref[rs, d:] = ext.astype(BF16)


def _route_kernel(lg_ref, slot_ref, off_ref, *, pad, tp, cap):
    ne, nc, _ = lg_ref.shape
    lg = lg_ref[...]
    m = jnp.max(lg, axis=0, keepdims=True)
    ex = jnp.exp(lg - m)
    aff = ex / jnp.sum(ex, axis=0, keepdims=True)
    tpos = (lax.broadcasted_iota(jnp.int32, (nc, LANES), 0) * LANES
            + lax.broadcasted_iota(jnp.int32, (nc, LANES), 1))
    valid = jnp.logical_and(tpos >= pad, tpos < tp)[None]
    affm = jnp.where(valid, aff, -1.0)

    def count(mask):
        c = jnp.sum(jnp.where(mask, 1.0, 0.0), axis=1, keepdims=True)
        return jnp.sum(c, axis=2, keepdims=True)

    def as_float(bits):
        return lax.bitcast_convert_type(bits, F32)

    def bs_body(_, lohi):
        lo, hi = lohi
        mid = lo + lax.shift_right_logical(hi - lo, 1)
        ok = count(affm >= as_float(mid)) >= float(cap)
        return jnp.where(ok, mid, lo), jnp.where(ok, hi, mid)

    lo0 = jnp.zeros((ne, 1, 1), jnp.int32)
    hi0 = jnp.full((ne, 1, 1), 0x3F800001, jnp.int32)
    thr_bits, _ = lax.fori_loop(0, 31, bs_body, (lo0, hi0))
    thr = as_float(thr_bits)

    gt = affm > thr
    eq = affm == thr
    need = float(cap) - count(gt)

    ci = lax.broadcasted_iota(jnp.int32, (LANES, LANES), 0)
    cj = lax.broadcasted_iota(jnp.int32, (LANES, LANES), 1)
    tri = (ci <= cj).astype(BF16)
    ri = lax.broadcasted_iota(jnp.int32, (nc, nc), 0)
    rj = lax.broadcasted_iota(jnp.int32, (nc, nc), 1)
    ltri = (rj < ri).astype(BF16)

    def prefix(mask2d):
        within = jnp.dot(mask2d.astype(BF16), tri, preferred_element_type=F32)
        tot = jnp.broadcast_to(within[:, LANES - 1:LANES], (nc, LANES)).astype(BF16)
        off = jnp.dot(ltri, tot, preferred_element_type=F32)
        return within + off, off

    for e in range(ne):
        tie_incl, _ = prefix(eq[e])
        tie_excl = tie_incl - eq[e].astype(F32)
        sel = jnp.logical_or(gt[e], jnp.logical_and(eq[e], tie_excl < need[e]))
        cum, off = prefix(sel)
        slot_ref[e] = jnp.where(sel, cum.astype(jnp.int32) - 1, -1)
        off_ref[e] = off.astype(jnp.int32)


def _dispatch_kernel(start_ref, h2e_ref, slot_ref, xs_ref, *, tcd, rw, s_rows, nc1, ne):
    b = pl.program_id(0)
    g = pl.program_id(1)
    i = pl.program_id(2)
    ge = slot_ref.shape[0]
    dext = h2e_ref.shape[1]

    @pl.when(i == 0)
    def _():
        xs_ref[...] = jnp.zeros_like(xs_ref)

    rw1, rw2 = rw
    for c in range(0, tcd, 2):
        gc = min(2, tcd - c)
        rw = rw2 if gc == 2 else rw1
        r_iota = lax.broadcasted_iota(jnp.int32, (rw, 1), 0)
        base = (b * nc1 + i * tcd + c) * ne + g * ge
        s1 = [start_ref[base + gc * ne + e] for e in range(ge)]
        lo = [(start_ref[base + e] // BF16_ROWS) * BF16_ROWS for e in range(ge)]
        npass = functools.reduce(jnp.maximum, [(s1[e] - lo[e] + rw - 1) // rw for e in range(ge)])
        rows = slice(c * CHUNK, (c + gc) * CHUNK)

        def one_pass(q, lo=lo, rows=rows, rw=rw, r_iota=r_iota):
            ws = [pl.multiple_of(jnp.minimum(lo[e] + q * rw, s_rows - rw), BF16_ROWS) for e in range(ge)]
            hits = []
            for e in range(ge):
                slot_row = slot_ref[e:e + 1, rows]
                hit = (slot_row - ws[e]) == r_iota
                if not isinstance(q, int) or q != 0:
                    hit = jnp.logical_and(hit, slot_row >= lo[e] + q * rw)
                hits.append(hit.astype(BF16))
            pc = jnp.concatenate(hits, axis=0)
            for n in range(0, dext, MXU_DIM):
                cs = slice(n, min(n + MXU_DIM, dext))
                res = jnp.dot(pc, h2e_ref[rows, cs], preferred_element_type=F32)
                for e in range(ge):
                    xs_ref[e, pl.ds(ws[e], rw), cs] += res[e * rw:(e + 1) * rw].astype(BF16)

        one_pass(0)

        def extra_pass(q, carry, one_pass=one_pass):
            one_pass(q)
            return carry

        lax.fori_loop(1, npass, extra_pass, 0)


def _ffn_kernel(xs_ref, wg_ref, wu_ref, wd_ref, y_ref, wgb_ref, wub_ref, wdb_ref, yacc_ref, *, ne, ff_blk):
    e = pl.program_id(0)

    @pl.when(pl.program_id(1) == 0)
    def _():
        wgb_ref[...] = wg_ref[...].astype(BF16)
        wub_ref[...] = wu_ref[...].astype(BF16)
        wdb_ref[...] = wd_ref[...].astype(BF16)

    wg_ref, wu_ref, wd_ref = wgb_ref, wub_ref, wdb_ref
    d = wg_ref.shape[0]
    xs = xs_ref[:, 0:d]
    ext = xs_ref[:, d:].astype(F32)
    lane = lax.broadcasted_iota(jnp.int32, ext.shape, 1)
    gate = jnp.sum(jnp.where(jnp.bitwise_and(lane, ne - 1) == e, ext, 0.0), axis=1, keepdims=True)
    ff = wg_ref.shape[1]
    for f in range(ff // ff_blk):
        fs = slice(f * ff_blk, (f + 1) * ff_blk)
        a = jnp.dot(xs, wg_ref[:, fs], preferred_element_type=F32)
        u = jnp.dot(xs, wu_ref[:, fs], preferred_element_type=F32)
        act = (jax.nn.silu(a) * u).astype(BF16)
        contrib = jnp.dot(act, wd_ref[fs, :], preferred_element_type=F32)
        if f == 0:
            yacc_ref[...] = contrib
        else:
            yacc_ref[...] += contrib
    y_ref[...] = (yacc_ref[...] * gate).astype(BF16)


def _combine_kernel(start_ref, *refs, c0, cpw, rc, s_rows, nc1):
    hs2_refs = refs[0:cpw]
    slot_refs = refs[cpw:2 * cpw]
    y_hbm, fg_ref, out_ref, ybuf_ref, yextra_ref, sem, sem_x = refs[2 * cpw:]
    b = pl.program_id(0)
    cc = pl.program_id(1)
    n_c = pl.num_programs(1)
    ne = slot_refs[0].shape[0]
    d = ybuf_ref.shape[3]
    n = b * n_c + cc
    par = n % 2
    r_iota = lax.broadcasted_iota(jnp.int32, (rc, 1), 0)

    def window_lo(bb, chunk):
        base = (bb * nc1 + chunk) * ne
        return [(start_ref[base + e] // BF16_ROWS) * BF16_ROWS for e in range(ne)]

    def window_start(lo_e, q):
        return pl.multiple_of(jnp.minimum(lo_e + q * rc, s_rows - rc), BF16_ROWS)

    def first_copies(bb, chunk, buf):
        lo_ = window_lo(bb, chunk)
        return [pltpu.make_async_copy(y_hbm.at[bb, e, pl.ds(window_start(lo_[e], 0), rc)],
                                      ybuf_ref.at[buf, e], sem.at[buf, e]) for e in range(ne)]

    @pl.when(n == 0)
    def _():
        for cp in first_copies(b, cc * cpw + c0, par):
            cp.start()

    @pl.when(n + 1 < pl.num_programs(0) * n_c)
    def _():
        n1 = n + 1
        for cp in first_copies(n1 // n_c, (n1 % n_c) * cpw + c0, 1 - par):
            cp.start()

    c = cc * cpw + c0
    base = (b * nc1 + c) * ne
    s1 = [start_ref[base + cpw * ne + e] for e in range(ne)]
    lo = window_lo(b, c)
    npass = functools.reduce(jnp.maximum, [(s1[e] - lo[e] + rc - 1) // rc for e in range(ne)])

    def onehots(q):
        hits = []
        for e in range(ne):
            slot_row = jnp.concatenate([slot_refs[k][e:e + 1, :] for k in range(cpw)], axis=1)
            hit = (slot_row - window_start(lo[e], q)) == r_iota
            if not isinstance(q, int) or q != 0:
                hit = jnp.logical_and(hit, slot_row >= lo[e] + q * rc)
            hits.append(hit.astype(BF16))
        return jnp.concatenate(hits, axis=0)

    def tn_dot(pc, yy):
        return lax.dot_general(pc, yy, (((0,), (0,)), ((), ())), preferred_element_type=F32)

    pc0 = onehots(0)
    for cp in first_copies(b, c, par):
        cp.wait()
    acc0 = tn_dot(pc0, ybuf_ref[par].reshape(ne * rc, d))

    def extra_pass(q, acc):
        cps = [pltpu.make_async_copy(y_hbm.at[b, e, pl.ds(window_start(lo[e], q), rc)],
                                     yextra_ref.at[e], sem_x.at[e]) for e in range(ne)]
        for cp in cps:
            cp.start()
        pcq = onehots(q)
        for cp in cps:
            cp.wait()
        return acc + tn_dot(pcq, yextra_ref[...].reshape(ne * rc, d))

    acc = lax.fori_loop(1, npass, extra_pass, acc0)
    for k in range(cpw):
        hs3 = hs2_refs[k][...].astype(F32) + acc[k * CHUNK:(k + 1) * CHUNK, :]
        ms = jnp.mean(hs3 * hs3, axis=-1, keepdims=True)
        out_ref[k * CHUNK:(k + 1) * CHUNK, :] = hs3 * lax.rsqrt(ms + EPS) * fg_ref[...]


def _vmem(mb):
    return mb * 1024 * 1024


def kernel(x, meta_tokens, norm1_g, w_in, conv_w, conv_b, lru_w_r, lru_b_r, lru_w_i, lru_b_i,
           lru_lambda, lru_out_g, ret_out_g, w_out, norm2_g, w_router, w_gate, w_up, w_down, final_g):
    bsz, seq, d = x.shape
    depth = norm1_g.shape[0]
    assert depth == 1
    n_meta = meta_tokens.shape[0]
    t_len = seq + n_meta
    pad = (-t_len) % CHUNK
    tp = t_len + pad
    nc = tp // CHUNK
    assert (pad + n_meta) % CHUNK == 0
    c0 = (pad + n_meta) // CHUNK
    rows = bsz * tp
    w_lru = conv_w.shape[-1]
    w_ret = ret_out_g.shape[-1]
    in_cols = w_in.shape[-1]
    cb = w_lru
    assert w_ret == cb and in_cols == 6 * cb and cb % MXU_DIM == 0
    hd = w_ret // RET_HEADS
    ne = w_router.shape[-1]
    ff = w_gate.shape[-1]
    cap = EC_CAPACITY * t_len // ne

    tc = _largest_divisor_leq(nc, 5)
    tm = tc * CHUNK
    n_t = nc // tc
    n_tiles = rows // tm

    assert c0 == 1 and seq % CHUNK == 0
    head = jnp.concatenate([jnp.zeros((pad, d), F32), meta_tokens.astype(F32)], axis=0)
    x4 = x.astype(F32).reshape(bsz, seq // CHUNK, CHUNK, d)

    def x_piece_spec(k):
        return pl.BlockSpec((None, None, CHUNK, d),
                            lambda b, j: (b, jnp.maximum(j * tc + k - 1, 0), 0, 0))

    row_tile_specs = [pl.BlockSpec((CHUNK, d), lambda b, j: (0, 0))] + [x_piece_spec(k) for k in range(tc)]

    half = hd // 2
    freqs = ROPE_BASE ** (-jnp.arange(half, dtype=F32) / half)
    ang = (jnp.arange(tp) - pad).astype(F32)[:, None] * freqs[None, :]
    cos2 = jnp.concatenate([jnp.cos(ang), jnp.cos(ang)], axis=1)
    sin2 = jnp.concatenate([-jnp.sin(ang), jnp.sin(ang)], axis=1)

    w_in_b = w_in[0].astype(BF16)
    w_out_b = w_out[0].astype(BF16)

    z = pl.pallas_call(
        functools.partial(_inproj_kernel, tc=tc, cb=cb, hd=hd, q_scale=float(hd) ** -0.5),
        grid=(bsz, n_t),
        in_specs=row_tile_specs + [
            pl.BlockSpec((1, d), lambda b, j: (0, 0)),
            pl.BlockSpec((d, in_cols), lambda b, j: (0, 0)),
            pl.BlockSpec((tm, hd), lambda b, j: (j, 0)),
            pl.BlockSpec((tm, hd), lambda b, j: (j, 0)),
        ],
        out_specs=pl.BlockSpec((tm, in_cols), lambda b, j: (b * n_t + j, 0)),
        out_shape=jax.ShapeDtypeStruct((rows, in_cols), BF16),
        scratch_shapes=[pltpu.VMEM((tm, d), F32)],
        compiler_params=pltpu.CompilerParams(dimension_semantics=("arbitrary",) * 2,
                                             vmem_limit_bytes=_vmem(52)),
        name="inproj",
    )(head, *([x4] * tc), norm1_g[0][None], w_in_b, cos2, sin2)

    lb = w_lru // LRU_BLOCKS
    per_tile = MXU_DIM // lb
    eye = jnp.eye(per_tile, dtype=F32)

    def blockdiag(wb):
        wb = wb.reshape(2, w_lru // MXU_DIM, per_tile, lb, lb)
        return jnp.einsum('dhpij,pq->dhpiqj', wb, eye).reshape(2, w_lru // MXU_DIM, MXU_DIM, MXU_DIM).astype(BF16)

    wr_bd = blockdiag(0.5 * lru_w_r[0])
    wi_bd = blockdiag(0.5 * lru_w_i[0])
    t_l = max(t for t in range(32, min(tp, 512) + 1, 32) if tp % t == 0 and (t // 32) % 2 == 1)
    n_l = tp // t_l
    seg = t_l // SUBLANES
    assert seg >= CONV_WIDTH and cb % LANES == 0
    nsub = _largest_divisor_leq(n_l, 5)
    t_b = nsub * t_l
    n_lb = n_l // nsub
    hb8 = t_b // BF16_ROWS
    last8 = rows // BF16_ROWS - 1

    def ltile_of(b, p, j):
        return b * n_lb + jnp.where(p == 0, n_lb - 1 - j, j)

    y_lru = pl.pallas_call(
        functools.partial(_lru_kernel, n_t=n_l, t_t=t_l, pad=pad),
        grid=(bsz, 2, n_lb),
        in_specs=[
            pl.BlockSpec((t_b, cb), lambda b, p, j: (ltile_of(b, p, j), 0)),
            pl.BlockSpec((BF16_ROWS, cb), lambda b, p, j: (jnp.maximum(ltile_of(b, p, j) * hb8 - 1, 0), 0)),
            pl.BlockSpec((BF16_ROWS, cb), lambda b, p, j: (jnp.minimum((ltile_of(b, p, j) + 1) * hb8, last8), 0)),
            pl.BlockSpec((t_b, cb), lambda b, p, j: (b * n_lb + p * j, 1)),
            pl.BlockSpec((CONV_WIDTH, cb), lambda b, p, j: (0, 0)),
            pl.BlockSpec((1, cb), lambda b, p, j: (0, 0)),
            pl.BlockSpec((None, w_lru // MXU_DIM, MXU_DIM, MXU_DIM), lambda b, p, j: (1 - p, 0, 0, 0)),
            pl.BlockSpec((None, w_lru // MXU_DIM, MXU_DIM, MXU_DIM), lambda b, p, j: (1 - p, 0, 0, 0)),
            pl.BlockSpec((None, 1, cb), lambda b, p, j: (1 - p, 0, 0)),
            pl.BlockSpec((None, 1, cb), lambda b, p, j: (1 - p, 0, 0)),
            pl.BlockSpec((None, 1, cb), lambda b, p, j: (1 - p, 0, 0)),
            pl.BlockSpec((1, cb), lambda b, p, j: (0, 0)),
        ],
        out_specs=pl.BlockSpec((t_b, cb), lambda b, p, j: (b * n_lb + p * j, 0)),
        out_shape=jax.ShapeDtypeStruct((rows, cb), BF16),
        scratch_shapes=[
            pltpu.VMEM((n_l, seg, SUBLANES, cb), F32),
            pltpu.VMEM((cb // LANES, t_l, LANES), F32),
            pltpu.VMEM((seg + CONV_WIDTH - 1, SUBLANES, cb), F32),
            pltpu.VMEM((seg, SUBLANES, cb), F32),
            pltpu.VMEM((seg, SUBLANES, cb), F32),
            pltpu.VMEM((seg, SUBLANES, cb), F32),
            pltpu.VMEM((seg, SUBLANES, cb), F32),
            pltpu.VMEM((1, cb), F32),
        ],
        compiler_params=pltpu.CompilerParams(dimension_semantics=("arbitrary",) * 3,
                                             vmem_limit_bytes=_vmem(52)),
        name="lru",
    )(z, z, z, z, conv_w[0], conv_b[0][None], wr_bd, wi_bd, 0.5 * lru_b_r[0][:, None], 0.5 * lru_b_i[0][:, None],
      lru_lambda[0][:, None], lru_out_g[0][None])

    log_g = jnp.log(1.0 - jnp.exp2(-5.0 - jnp.arange(RET_HEADS, dtype=F32)))
    idx = jnp.arange(CHUNK, dtype=F32)
    dmat = jnp.exp(log_g[:, None, None] * jnp.abs(idx[:, None] - idx[None, :]))

    def posdec(expo):
        return jnp.broadcast_to(jnp.exp(log_g[:, None] * expo[None, :])[:, :, None], (RET_HEADS, CHUNK, hd))

    dvec = jnp.stack([posdec(idx + 1.0), posdec(CHUNK - idx), posdec(CHUNK - 1.0 - idx), posdec(idx)])
    cdm = jnp.broadcast_to(jnp.exp(log_g * CHUNK)[:, None, None], (RET_HEADS, hd, hd))

    tc_r = _largest_divisor_leq(nc, 13)
    tm_r = tc_r * CHUNK
    n_r = nc // tc_r

    def rtile_of(b, p, j):
        return b * n_r + jnp.where(p == 0, n_r - 1 - j, j)

    y_ret = pl.pallas_call(
        functools.partial(_retention_kernel, n_t=n_r, tc=tc_r, hd=hd),
        grid=(bsz, 2, n_r),
        in_specs=[
            pl.BlockSpec((tm_r, cb), lambda b, p, j: (b * n_r + p * j, 2)),
            pl.BlockSpec((tm_r, cb), lambda b, p, j: (rtile_of(b, p, j), 3)),
            pl.BlockSpec((tm_r, cb), lambda b, p, j: (rtile_of(b, p, j), 4)),
            pl.BlockSpec((tm_r, cb), lambda b, p, j: (b * n_r + p * j, 5)),
            pl.BlockSpec((RET_HEADS, CHUNK, CHUNK), lambda b, p, j: (0, 0, 0)),
            pl.BlockSpec((4, RET_HEADS, CHUNK, hd), lambda b, p, j: (0, 0, 0, 0)),
            pl.BlockSpec((RET_HEADS, hd, hd), lambda b, p, j: (0, 0, 0)),
            pl.BlockSpec((1, cb), lambda b, p, j: (0, 0)),
        ],
        out_specs=pl.BlockSpec((tm_r, cb), lambda b, p, j: (b * n_r + p * j, 0)),
        out_shape=jax.ShapeDtypeStruct((rows, cb), BF16),
        scratch_shapes=[
            pltpu.VMEM((nc, RET_HEADS, hd, hd), BF16),
            pltpu.VMEM((RET_HEADS, hd, hd), F32),
        ],
        compiler_params=pltpu.CompilerParams(dimension_semantics=("arbitrary",) * 3,
                                             vmem_limit_bytes=_vmem(52)),
        name="retention",
    )(z, z, z, z, dmat, dvec, cdm, ret_out_g[0][None])

    assert ne & (ne - 1) == 0 and 3 * ne <= LANES
    dext = d + LANES
    w_router_pad = jnp.pad(w_router[0].astype(F32), ((0, 0), (0, LANES - ne)))
    wr_hi = w_router_pad.astype(BF16)
    wr_mid = (w_router_pad - wr_hi.astype(F32)).astype(BF16)
    hs2, h2e, logits_t = pl.pallas_call(
        functools.partial(_outproj_kernel, tc=tc, ne=ne),
        grid=(bsz, n_t),
        in_specs=[
            pl.BlockSpec((tm, cb), lambda b, j: (b * n_t + j, 0)),
            pl.BlockSpec((tm, cb), lambda b, j: (b * n_t + j, 0)),
        ] + row_tile_specs + [
            pl.BlockSpec((d, d), lambda b, j: (0, 0)),
            pl.BlockSpec((1, d), lambda b, j: (0, 0)),
            pl.BlockSpec((d, 2 * LANES), lambda b, j: (0, 0)),
        ],
        out_specs=[
            pl.BlockSpec((tm, d), lambda b, j: (b * n_t + j, 0)),
            pl.BlockSpec((tm, dext), lambda b, j: (b * n_t + j, 0)),
            pl.BlockSpec((None, ne, tm), lambda b, j: (b, 0, j)),
        ],
        out_shape=[
            jax.ShapeDtypeStruct((rows, d), BF16),
            jax.ShapeDtypeStruct((rows, dext), BF16),
            jax.ShapeDtypeStruct((bsz, ne, tp), F32),
        ],
        scratch_shapes=[pltpu.VMEM((tm, d), F32)],
        compiler_params=pltpu.CompilerParams(dimension_semantics=("arbitrary",) * 2,
                                             vmem_limit_bytes=_vmem(52)),
        name="outproj",
    )(y_lru, y_ret, head, *([x4] * tc), w_out_b, norm2_g[0][None],
      jnp.concatenate([wr_hi, wr_mid], axis=1))

    ncp = _round_up(nc, BF16_ROWS)
    logits4 = jnp.pad(logits_t.reshape(bsz, ne, nc, LANES), ((0, 0), (0, 0), (0, ncp - nc), (0, 0)))
    slot4, off4 = pl.pallas_call(
        functools.partial(_route_kernel, pad=pad, tp=tp, cap=cap),
        grid=(bsz,),
        in_specs=[pl.BlockSpec((None, ne, ncp, LANES), lambda b: (b, 0, 0, 0))],
        out_specs=[pl.BlockSpec((None, ne, ncp, LANES), lambda b: (b, 0, 0, 0))] * 2,
        out_shape=[
            jax.ShapeDtypeStruct((bsz, ne, ncp, LANES), jnp.int32),
            jax.ShapeDtypeStruct((bsz, ne, ncp, LANES), jnp.int32),
        ],
        compiler_params=pltpu.CompilerParams(dimension_semantics=("arbitrary",),
                                             vmem_limit_bytes=_vmem(52)),
        name="route",
    )(logits4)

    start = jnp.concatenate([jnp.transpose(off4[:, :, :nc, 0], (0, 2, 1)),
                             jnp.full((bsz, 1, ne), cap, jnp.int32)], axis=1).reshape(-1)
    nc1 = nc + 1
    slot_rows = slot4[:, :, :nc].reshape(bsz, ne, tp)

    s_rows = _round_up(cap, BF16_ROWS)
    tcd = _largest_divisor_leq(nc, 13)
    w_d = tcd * CHUNK
    n_d = nc // tcd
    rw = (DISPATCH_WINDOW, DISPATCH_WINDOW_PAIR)
    ge = min(ne, SUBLANES)
    assert all(s_rows >= r and (s_rows - r) % BF16_ROWS == 0 for r in rw) and ne % ge == 0
    xs = pl.pallas_call(
        functools.partial(_dispatch_kernel, tcd=tcd, rw=rw, s_rows=s_rows, nc1=nc1, ne=ne),
        grid_spec=pltpu.PrefetchScalarGridSpec(
            num_scalar_prefetch=1,
            grid=(bsz, ne // ge, n_d),
            in_specs=[
                pl.BlockSpec((w_d, dext), lambda b, g, i, st: (b * n_d + i, 0)),
                pl.BlockSpec((None, ge, w_d), lambda b, g, i, st: (b, g, i)),
            ],
            out_specs=pl.BlockSpec((None, ge, s_rows, dext), lambda b, g, i, st: (b, g, 0, 0)),
        ),
        out_shape=jax.ShapeDtypeStruct((bsz, ne, s_rows, dext), BF16),
        compiler_params=pltpu.CompilerParams(dimension_semantics=("arbitrary",) * 3,
                                             vmem_limit_bytes=_vmem(52)),
        name="dispatch",
    )(start, h2e, slot_rows)

    ff_blk = 256 if ff % 256 == 0 else ff
    y_e = pl.pallas_call(
        functools.partial(_ffn_kernel, ne=ne, ff_blk=ff_blk),
        grid=(ne, bsz),
        in_specs=[
            pl.BlockSpec((None, None, s_rows, dext), lambda e, b: (b, e, 0, 0)),
            pl.BlockSpec((None, d, ff), lambda e, b: (e, 0, 0)),
            pl.BlockSpec((None, d, ff), lambda e, b: (e, 0, 0)),
            pl.BlockSpec((None, ff, d), lambda e, b: (e, 0, 0)),
        ],
        out_specs=pl.BlockSpec((None, None, s_rows, d), lambda e, b: (b, e, 0, 0)),
        out_shape=jax.ShapeDtypeStruct((bsz, ne, s_rows, d), BF16),
        scratch_shapes=[
            pltpu.VMEM((d, ff), BF16),
            pltpu.VMEM((d, ff), BF16),
            pltpu.VMEM((ff, d), BF16),
            pltpu.VMEM((s_rows, d), F32),
        ],
        compiler_params=pltpu.CompilerParams(dimension_semantics=("arbitrary",) * 2,
                                             vmem_limit_bytes=_vmem(56)),
        name="ffn",
    )(xs, w_gate[0], w_up[0], w_down[0])

    rc = COMBINE_WINDOW
    cpw = COMBINE_CHUNKS if (nc - c0) % COMBINE_CHUNKS == 0 else 1
    assert (s_rows - rc) % BF16_ROWS == 0
    slot_t = jnp.transpose(slot4[:, :, :nc], (0, 2, 1, 3))
    out = pl.pallas_call(
        functools.partial(_combine_kernel, c0=c0, cpw=cpw, rc=rc, s_rows=s_rows, nc1=nc1),
        grid_spec=pltpu.PrefetchScalarGridSpec(
            num_scalar_prefetch=1,
            grid=(bsz, (nc - c0) // cpw),
            in_specs=(
                [pl.BlockSpec((CHUNK, d), functools.partial(
                    lambda b, c, st, k: (b * nc + c * cpw + c0 + k, 0), k=k)) for k in range(cpw)]
                + [pl.BlockSpec((None, None, ne, LANES), functools.partial(
                    lambda b, c, st, k: (b, c * cpw + c0 + k, 0, 0), k=k)) for k in range(cpw)]
                + [pl.BlockSpec(memory_space=pl.ANY),
                   pl.BlockSpec((1, d), lambda b, c, st: (0, 0))]),
            out_specs=pl.BlockSpec((None, cpw * CHUNK, d), lambda b, c, st: (b, c, 0)),
            scratch_shapes=[
                pltpu.VMEM((2, ne, rc, d), BF16),
                pltpu.VMEM((ne, rc, d), BF16),
                pltpu.SemaphoreType.DMA((2, ne)),
                pltpu.SemaphoreType.DMA((ne,)),
            ],
        ),
        out_shape=jax.ShapeDtypeStruct((bsz, seq, d), x.dtype),
        compiler_params=pltpu.CompilerParams(dimension_semantics=("arbitrary",) * 2,
                                             vmem_limit_bytes=_vmem(32)),
        name="combine",
    )(start, *([hs2] * cpw), *([slot_t] * cpw), y_e, final_g[None])
    return out
```

```python
import functools

import jax
import jax.numpy as jnp
from jax import lax
from jax.experimental import pallas as pl
from jax.experimental.pallas import tpu as pltpu

LRU_BLOCKS = 8
LRU_C = 8.0
CONV_WIDTH = 4
CONV_LEFT = 2
RET_HEADS = 4
CHUNK = 128
ROPE_BASE = 10000.0
EC_CAPACITY = 2
EPS = 1e-6

LANES = 128
SUBLANES = 8
BF16_ROWS = 16
MXU_DIM = 256
V7X_VMEM_BYTES = 64 * 1024 * 1024
VMEM_LIMIT = V7X_VMEM_BYTES * 13 // 16

DISPATCH_WINDOW = 3 * BF16_ROWS
DISPATCH_WINDOW_PAIR = 4 * BF16_ROWS
COMBINE_WINDOW = 4 * BF16_ROWS
COMBINE_CHUNKS = 2

F32 = jnp.float32
BF16 = jnp.bfloat16


def _round_up(x, m):
    return (x + m - 1) // m * m


def _largest_divisor_leq(n, k):
    for d in range(min(n, k), 0, -1):
        if n % d == 0:
            return d
    return 1


def _gather_row_tile(head_ref, x_refs, dst_ref):
    first = pl.program_id(1) == 0
    for k, xr in enumerate(x_refs):
        piece = xr[...]
        if k == 0:
            piece = jnp.where(first, head_ref[...], piece)
        dst_ref[k * CHUNK:(k + 1) * CHUNK, :] = piece


def _inproj_kernel(head_ref, *refs, tc, cb, hd, q_scale):
    x_refs = refs[0:tc]
    g_ref, w_ref, cos_ref, sin_ref, z_ref, xt_ref = refs[tc:]
    _gather_row_tile(head_ref, x_refs, xt_ref)
    x = xt_ref[...]
    ms = jnp.mean(x * x, axis=-1, keepdims=True)
    h = (x * lax.rsqrt(ms + EPS) * g_ref[...]).astype(BF16)
    cos = cos_ref[...]
    sin = sin_ref[...]
    for j in range(w_ref.shape[1] // cb):
        zj = jnp.dot(h, w_ref[:, j * cb:(j + 1) * cb], preferred_element_type=F32)
        if j in (2, 3):
            for hh in range(cb // hd):
                t = zj[:, hh * hd:(hh + 1) * hd]
                r = pltpu.roll(t, hd // 2, axis=1)
                t = t * cos + r * sin
                if j == 2:
                    t = t * q_scale
                z_ref[:, j * cb + hh * hd:j * cb + (hh + 1) * hd] = t.astype(z_ref.dtype)
        else:
            if j == 1:
                zj = jax.nn.gelu(zj, approximate=True)
            elif j == 5:
                zj = jax.nn.silu(zj)
            z_ref[:, j * cb:(j + 1) * cb] = zj.astype(z_ref.dtype)


def _interleave_in(src_ref, r0, slab_ref, dst_ref, seg, off):
    nsl, t_t = slab_ref.shape[0], slab_ref.shape[1]
    for c in range(nsl):
        slab_ref[c] = src_ref[pl.ds(r0, t_t), c * LANES:(c + 1) * LANES].astype(F32)
    for j in range(seg):
        dst_ref[off + j] = jnp.concatenate(
            [slab_ref[c, pl.ds(j, SUBLANES, stride=seg), :] for c in range(nsl)], axis=1)


def _interleave_out(src_ref, slab_ref, dst_ref, r0, seg):
    nsl, t_t = slab_ref.shape[0], slab_ref.shape[1]
    for j in range(seg):
        row = src_ref[j]
        for c in range(nsl):
            slab_ref[c, pl.ds(j, SUBLANES, stride=seg), :] = row[:, c * LANES:(c + 1) * LANES]
    for c in range(nsl):
        dst_ref[pl.ds(r0, t_t), c * LANES:(c + 1) * LANES] = slab_ref[c].astype(dst_ref.dtype)


def _scan_tile(a_ref, b_ref, h_ref, acc_ref, carry_ref, seg, forward):
    w = a_ref.shape[2]

    unroll = 4 if seg % 4 == 0 else 1

    def body(jb, carry):
        h, acc = carry
        for jo in range(unroll):
            jj = jb * unroll + jo
            j = jj if forward else seg - 1 - jj
            a = a_ref[j]
            h = a * h + b_ref[j]
            acc = a * acc
            h_ref[j] = h
            acc_ref[j] = acc
        return h, acc

    h_end, a_end = lax.fori_loop(0, seg // unroll, body,
                                 (jnp.zeros((SUBLANES, w), F32), jnp.ones((SUBLANES, w), F32)))
    c = carry_ref[...]
    rows = [None] * SUBLANES
    order = range(SUBLANES) if forward else range(SUBLANES - 1, -1, -1)
    for s in order:
        rows[s] = c
        c = a_end[s:s + 1, :] * c + h_end[s:s + 1, :]
    carry_ref[...] = c
    cvec = jnp.concatenate(rows, axis=0)
    return h_ref[...] + acc_ref[...] * cvec[None]


def _lru_kernel(xm_ref, xp_ref, xn_ref, gate_ref, cw_ref, cb_ref, wr_ref, wi_ref, br_ref, bi_ref,
                lam_ref, og_ref, y_ref, hb_ref, slab_ref, xext_ref, a_ref, b_ref, h_ref, acc_ref, carry_ref,
                *, n_t, t_t, pad):
    p = pl.program_id(1)

    @pl.when(pl.program_id(2) == 0)
    def _():
        carry_ref[...] = jnp.zeros_like(carry_ref)

    nsub = xm_ref.shape[0] // t_t

    def sub_tile(k, carry):
        kk = jnp.where(p == 0, nsub - 1 - k, k)
        blk = jnp.where(p == 0, pl.num_programs(2) - 1 - pl.program_id(2), pl.program_id(2))
        _lru_tile(kk, blk * nsub + kk, xm_ref, xp_ref, xn_ref, gate_ref, cw_ref, cb_ref, wr_ref, wi_ref,
                  br_ref, bi_ref, lam_ref, og_ref, y_ref, hb_ref, slab_ref, xext_ref, a_ref, b_ref, h_ref,
                  acc_ref, carry_ref, n_t=n_t, t_t=t_t, pad=pad)
        return carry

    lax.fori_loop(0, nsub, sub_tile, 0)


def _lru_tile(kk, ti, xm_ref, xp_ref, xn_ref, gate_ref, cw_ref, cb_ref, wr_ref, wi_ref, br_ref, bi_ref,
              lam_ref, og_ref, y_ref, hb_ref, slab_ref, xext_ref, a_ref, b_ref, h_ref, acc_ref, carry_ref,
              *, n_t, t_t, pad):
    p = pl.program_id(1)
    w = xm_ref.shape[1]
    seg = t_t // SUBLANES
    nsub = xm_ref.shape[0] // t_t
    nhalo = xp_ref.shape[0]
    r0 = pl.multiple_of(kk * t_t, BF16_ROWS)

    _interleave_in(xm_ref, r0, slab_ref, xext_ref, seg, CONV_LEFT)
    sub = lax.broadcasted_iota(jnp.int32, (SUBLANES, w), 0)
    before = xm_ref[pl.ds(pl.multiple_of(jnp.maximum(r0 - nhalo, 0), BF16_ROWS), nhalo), :]
    after = xm_ref[pl.ds(pl.multiple_of(jnp.minimum(r0 + t_t, (nsub - 1) * t_t), BF16_ROWS), nhalo), :]
    xp8 = jnp.where(kk == 0, xp_ref[...], before).astype(F32)
    xn8 = jnp.where(kk == nsub - 1, xn_ref[...], after).astype(F32)
    xp8 = jnp.where(ti == 0, 0.0, xp8)
    xn8 = jnp.where(ti == n_t - 1, 0.0, xn8)
    for back in range(1, CONV_LEFT + 1):
        xext_ref[CONV_LEFT - back] = jnp.where(
            sub == 0, xp8[nhalo - back:nhalo - back + 1, :],
            pltpu.roll(xext_ref[CONV_LEFT + seg - back], 1, axis=0))
    for fwd in range(CONV_WIDTH - 1 - CONV_LEFT):
        xext_ref[CONV_LEFT + seg + fwd] = jnp.where(
            sub == SUBLANES - 1, xn8[fwd:fwd + 1, :],
            pltpu.roll(xext_ref[CONV_LEFT + fwd], SUBLANES - 1, axis=0))
    u3 = jnp.zeros((seg, SUBLANES, w), F32) + cb_ref[...]
    for k in range(CONV_WIDTH):
        u3 = u3 + cw_ref[k:k + 1, :] * xext_ref[k:k + seg]
    u = u3.reshape(t_t, w)

    ub = u.astype(BF16)
    nh = w // MXU_DIM
    pre_r = jnp.concatenate(
        [jnp.dot(ub[:, q * MXU_DIM:(q + 1) * MXU_DIM], wr_ref[q], preferred_element_type=F32)
         for q in range(nh)], axis=1) + br_ref[...]
    pre_i = jnp.concatenate(
        [jnp.dot(ub[:, q * MXU_DIM:(q + 1) * MXU_DIM], wi_ref[q], preferred_element_type=F32)
         for q in range(nh)], axis=1) + bi_ref[...]
    tr = jnp.tanh(pre_r)
    i = 0.5 + 0.5 * jnp.tanh(pre_i)
    nl = -lam_ref[...]
    en = jnp.exp(-jnp.abs(nl))
    w1 = 1.0 + en
    log1p_en = jnp.where(w1 == 1.0, en, en * jnp.log(w1) / jnp.where(w1 == 1.0, 1.0, w1 - 1.0))
    half_c = (-0.5 * LRU_C) * (jnp.maximum(nl, 0.0) + log1p_en)
    log_a = half_c + half_c * tr
    a = jnp.exp(log_a)
    v = -jnp.tanh(log_a) * (1.0 + a * a)
    gx = jnp.where(v > 0.0, v * lax.rsqrt(v), 0.0) * (i * u)
    a_ref[...] = a.reshape(seg, SUBLANES, w)
    b_ref[...] = gx.reshape(seg, SUBLANES, w)

    @pl.when(jnp.logical_and(p == 1, ti * t_t < pad))
    def _():
        j3 = lax.broadcasted_iota(jnp.int32, (seg, SUBLANES, 1), 0)
        s3 = lax.broadcasted_iota(jnp.int32, (seg, SUBLANES, 1), 1)
        rowidx = ti * t_t + s3 * seg + j3
        b_ref[...] = jnp.where(rowidx >= pad, b_ref[...], 0.0)

    @pl.when(p == 0)
    def _():
        hb_ref[ti] = _scan_tile(a_ref, b_ref, h_ref, acc_ref, carry_ref, seg, False)

    @pl.when(p == 1)
    def _():
        h = _scan_tile(a_ref, b_ref, h_ref, acc_ref, carry_ref, seg, True) + hb_ref[ti]
        _interleave_in(gate_ref, r0, slab_ref, b_ref, seg, 0)
        y = h * b_ref[...]
        ms = jnp.mean(y * y, axis=-1, keepdims=True)
        h_ref[...] = y * lax.rsqrt(ms + EPS) * og_ref[...]
        _interleave_out(h_ref, slab_ref, y_ref, r0, seg)


def _retention_kernel(q_ref, k_ref, v_ref, g_ref, dmat_ref, dvec_ref, cd_ref, og_ref, y_ref,
                      sb_ref, s_ref, *, n_t, tc, hd):
    p = pl.program_id(1)
    j = pl.program_id(2)
    ti = jnp.where(p == 0, n_t - 1 - j, j)
    nheads = q_ref.shape[1] // hd

    @pl.when(j == 0)
    def _():
        s_ref[...] = jnp.zeros_like(s_ref)

    def tn_dot(x, y):
        return lax.dot_general(x, y, (((0,), (0,)), ((), ())), preferred_element_type=F32)

    def nt_dot(x, y):
        return lax.dot_general(x, y, (((1,), (1,)), ((), ())), preferred_element_type=F32)

    @pl.when(p == 0)
    def _():
        for c in reversed(range(tc)):
            cg = ti * tc + c
            for h in range(nheads):
                rs = slice(c * CHUNK, (c + 1) * CHUNK)
                cs = slice(h * hd, (h + 1) * hd)
                kk = k_ref[rs, cs].astype(F32)
                vv = v_ref[rs, cs].astype(BF16)
                sb_ref[cg, h] = s_ref[h].astype(BF16)
                s_ref[h] = s_ref[h] * cd_ref[h] + tn_dot((kk * dvec_ref[3, h]).astype(BF16), vv)

    @pl.when(p == 1)
    def _():
        for c in range(tc):
            cg = ti * tc + c
            for h in range(nheads):
                rs = slice(c * CHUNK, (c + 1) * CHUNK)
                cs = slice(h * hd, (h + 1) * hd)
                qq = q_ref[rs, cs].astype(F32)
                kk = k_ref[rs, cs].astype(F32)
                vv = v_ref[rs, cs].astype(BF16)
                sc = nt_dot(qq.astype(BF16), kk.astype(BF16)) * dmat_ref[h]
                o = jnp.dot(sc.astype(BF16), vv, preferred_element_type=F32)
                o = o + jnp.dot((qq * dvec_ref[0, h]).astype(BF16), s_ref[h].astype(BF16),
                                preferred_element_type=F32)
                o = o + jnp.dot((qq * dvec_ref[1, h]).astype(BF16), sb_ref[cg, h],
                                preferred_element_type=F32)
                s_ref[h] = s_ref[h] * cd_ref[h] + tn_dot((kk * dvec_ref[2, h]).astype(BF16), vv)
                ms = jnp.mean(o * o, axis=-1, keepdims=True)
                o = o * lax.rsqrt(ms + EPS) * og_ref[:, cs]
                y_ref[rs, cs] = (o * g_ref[rs, cs].astype(F32)).astype(y_ref.dtype)


def _outproj_kernel(yl_ref, yr_ref, head_ref, *refs, tc, ne):
    x_refs = refs[0:tc]
    w_ref, g_ref, wr2_ref, hs2_ref, h2e_ref, lg_ref, xt_ref = refs[tc:]
    _gather_row_tile(head_ref, x_refs, xt_ref)
    d = xt_ref.shape[1]
    tm = xt_ref.shape[0]
    cut = ((tc + 1) // 2) * CHUNK
    for rs in ([slice(0, cut), slice(cut, tm)] if cut < tm else [slice(0, tm)]):
        ymix = jnp.concatenate([yl_ref[rs, :].astype(BF16), yr_ref[rs, :].astype(BF16)], axis=1)
        mix = jnp.dot(ymix, w_ref[...], preferred_element_type=F32)
        hs2 = xt_ref[rs, :] + mix
        hs2_ref[rs, :] = hs2.astype(hs2_ref.dtype)
        ms = jnp.mean(hs2 * hs2, axis=-1, keepdims=True)
        h2 = hs2 * lax.rsqrt(ms + EPS) * g_ref[...]
        hh = h2.astype(BF16)
        h2e_ref[rs, 0:d] = hh
        hm = (h2 - hh.astype(F32)).astype(BF16)
        w2 = wr2_ref[...]
        d_hi = jnp.dot(hh, w2, preferred_element_type=F32)
        d_mid = jnp.dot(hm, w2, preferred_element_type=F32)
        lr = d_hi[:, 0:LANES] + (d_hi[:, LANES:] + d_mid[:, 0:LANES])
        lg_ref[:, rs] = lr.T[0:ne, :]

        lane = lax.broadcasted_iota(jnp.int32, lr.shape, 1)
        live = lane < ne
        mx = jnp.max(jnp.where(live, lr, -jnp.inf), axis=1, keepdims=True)
        ex = jnp.where(live, jnp.exp(lr - mx), 0.0)
        aff = ex / jnp.sum(ex, axis=1, keepdims=True)
        a_hi = aff.astype(BF16).astype(F32)
        r1 = aff - a_hi
        a_mid = r1.astype(BF16).astype(F32)
        a_lo = (r1 - a_mid).astype(BF16).astype(F32)
        ext = a_hi + pltpu.roll(a_mid, ne, axis=1) + pltpu.roll(a_lo, 2 * ne, axis=1)
        h2e_ref[rs, d:] = ext.astype(BF16)


def _route_kernel(lg_ref, slot_ref, off_ref, *, pad, tp, cap):
    ne, nc, _ = lg_ref.shape
    lg = lg_ref[...]
    m = jnp.max(lg, axis=0, keepdims=True)
    ex = jnp.exp(lg - m)
    aff = ex / jnp.sum(ex, axis=0, keepdims=True)
    tpos = (lax.broadcasted_iota(jnp.int32, (nc, LANES), 0) * LANES
            + lax.broadcasted_iota(jnp.int32, (nc, LANES), 1))
    valid = jnp.logical_and(tpos >= pad, tpos < tp)[None]
    affm = jnp.where(valid, aff, -1.0)

    def count(mask):
        c = jnp.sum(jnp.where(mask, 1.0, 0.0), axis=1, keepdims=True)
        return jnp.sum(c, axis=2, keepdims=True)

    def as_float(bits):
        return lax.bitcast_convert_type(bits, F32)

    def bs_body(_, lohi):
        lo, hi = lohi
        mid = lo + lax.shift_right_logical(hi - lo, 1)
        ok = count(affm >= as_float(mid)) >= float(cap)
        return jnp.where(ok, mid, lo), jnp.where(ok, hi, mid)

    lo0 = jnp.zeros((ne, 1, 1), jnp.int32)
    hi0 = jnp.full((ne, 1, 1), 0x3F800001, jnp.int32)
    thr_bits, _ = lax.fori_loop(0, 31, bs_body, (lo0, hi0))
    thr = as_float(thr_bits)

    gt = affm > thr
    eq = affm == thr
    need = float(cap) - count(gt)

    ci = lax.broadcasted_iota(jnp.int32, (LANES, LANES), 0)
    cj = lax.broadcasted_iota(jnp.int32, (LANES, LANES), 1)
    tri = (ci <= cj).astype(BF16)
    ri = lax.broadcasted_iota(jnp.int32, (nc, nc), 0)
    rj = lax.broadcasted_iota(jnp.int32, (nc, nc), 1)
    ltri = (rj < ri).astype(BF16)

    def prefix(mask2d):
        within = jnp.dot(mask2d.astype(BF16), tri, preferred_element_type=F32)
        tot = jnp.broadcast_to(within[:, LANES - 1:LANES], (nc, LANES)).astype(BF16)
        off = jnp.dot(ltri, tot, preferred_element_type=F32)
        return within + off, off

    for e in range(ne):
        tie_incl, _ = prefix(eq[e])
        tie_excl = tie_incl - eq[e].astype(F32)
        sel = jnp.logical_or(gt[e], jnp.logical_and(eq[e], tie_excl < need[e]))
        cum, off = prefix(sel)
        slot_ref[e] = jnp.where(sel, cum.astype(jnp.int32) - 1, -1)
        off_ref[e] = off.astype(jnp.int32)


def _dispatch_kernel(start_ref, h2e_ref, slot_ref, xs_ref, *, tcd, rw, s_rows, nc1, ne):
    b = pl.program_id(0)
    g = pl.program_id(1)
    i = pl.program_id(2)
    ge = slot_ref.shape[0]
    dext = h2e_ref.shape[1]

    @pl.when(i == 0)
    def _():
        xs_ref[...] = jnp.zeros_like(xs_ref)

    rw1, rw2 = rw
    for c in range(0, tcd, 2):
        gc = min(2, tcd - c)
        rw = rw2 if gc == 2 else rw1
        r_iota = lax.broadcasted_iota(jnp.int32, (rw, 1), 0)
        base = (b * nc1 + i * tcd + c) * ne + g * ge
        s1 = [start_ref[base + gc * ne + e] for e in range(ge)]
        lo = [(start_ref[base + e] // BF16_ROWS) * BF16_ROWS for e in range(ge)]
        npass = functools.reduce(jnp.maximum, [(s1[e] - lo[e] + rw - 1) // rw for e in range(ge)])
        rows = slice(c * CHUNK, (c + gc) * CHUNK)

        def one_pass(q, lo=lo, rows=rows, rw=rw, r_iota=r_iota):
            ws = [pl.multiple_of(jnp.minimum(lo[e] + q * rw, s_rows - rw), BF16_ROWS) for e in range(ge)]
            hits = []
            for e in range(ge):
                slot_row = slot_ref[e:e + 1, rows]
                hit = (slot_row - ws[e]) == r_iota
                if not isinstance(q, int) or q != 0:
                    hit = jnp.logical_and(hit, slot_row >= lo[e] + q * rw)
                hits.append(hit.astype(BF16))
            pc = jnp.concatenate(hits, axis=0)
            for n in range(0, dext, MXU_DIM):
                cs = slice(n, min(n + MXU_DIM, dext))
                res = jnp.dot(pc, h2e_ref[rows, cs], preferred_element_type=F32)
                for e in range(ge):
                    xs_ref[e, pl.ds(ws[e], rw), cs] += res[e * rw:(e + 1) * rw].astype(BF16)

        one_pass(0)

        def extra_pass(q, carry, one_pass=one_pass):
            one_pass(q)
            return carry

        lax.fori_loop(1, npass, extra_pass, 0)


def _ffn_kernel(xs_ref, wg_ref, wu_ref, wd_ref, y_ref, wgb_ref, wub_ref, wdb_ref, yacc_ref, *, ne, ff_blk):
    e = pl.program_id(0)

    @pl.when(pl.program_id(1) == 0)
    def _():
        wgb_ref[...] = wg_ref[...].astype(BF16)
        wub_ref[...] = wu_ref[...].astype(BF16)
        wdb_ref[...] = wd_ref[...].astype(BF16)

    wg_ref, wu_ref, wd_ref = wgb_ref, wub_ref, wdb_ref
    d = wg_ref.shape[0]
    xs = xs_ref[:, 0:d]
    ext = xs_ref[:, d:].astype(F32)
    lane = lax.broadcasted_iota(jnp.int32, ext.shape, 1)
    gate = jnp.sum(jnp.where(jnp.bitwise_and(lane, ne - 1) == e, ext, 0.0), axis=1, keepdims=True)
    ff = wg_ref.shape[1]
    for f in range(ff // ff_blk):
        fs = slice(f * ff_blk, (f + 1) * ff_blk)
        a = jnp.dot(xs, wg_ref[:, fs], preferred_element_type=F32)
        u = jnp.dot(xs, wu_ref[:, fs], preferred_element_type=F32)
        act = (jax.nn.silu(a) * u).astype(BF16)
        contrib = jnp.dot(act, wd_ref[fs, :], preferred_element_type=F32)
        if f == 0:
            yacc_ref[...] = contrib
        else:
            yacc_ref[...] += contrib
    y_ref[...] = (yacc_ref[...] * gate).astype(BF16)


def _combine_kernel(start_ref, *refs, c0, cpw, rc, s_rows, nc1):
    hs2_refs = refs[0:cpw]
    slot_refs = refs[cpw:2 * cpw]
    y_hbm, fg_ref, out_ref, ybuf_ref, yextra_ref, sem, sem_x = refs[2 * cpw:]
    b = pl.program_id(0)
    cc = pl.program_id(1)
    n_c = pl.num_programs(1)
    ne = slot_refs[0].shape[0]
    d = ybuf_ref.shape[3]
    n = b * n_c + cc
    par = n % 2
    r_iota = lax.broadcasted_iota(jnp.int32, (rc, 1), 0)

    def window_lo(bb, chunk):
        base = (bb * nc1 + chunk) * ne
        return [(start_ref[base + e] // BF16_ROWS) * BF16_ROWS for e in range(ne)]

    def window_start(lo_e, q):
        return pl.multiple_of(jnp.minimum(lo_e + q * rc, s_rows - rc), BF16_ROWS)

    def first_copies(bb, chunk, buf):
        lo_ = window_lo(bb, chunk)
        return [pltpu.make_async_copy(y_hbm.at[bb, e, pl.ds(window_start(lo_[e], 0), rc)],
                                      ybuf_ref.at[buf, e], sem.at[buf, e]) for e in range(ne)]

    @pl.when(n == 0)
    def _():
        for cp in first_copies(b, cc * cpw + c0, par):
            cp.start()

    @pl.when(n + 1 < pl.num_programs(0) * n_c)
    def _():
        n1 = n + 1
        for cp in first_copies(n1 // n_c, (n1 % n_c) * cpw + c0, 1 - par):
            cp.start()

    c = cc * cpw + c0
    base = (b * nc1 + c) * ne
    s1 = [start_ref[base + cpw * ne + e] for e in range(ne)]
    lo = window_lo(b, c)
    npass = functools.reduce(jnp.maximum, [(s1[e] - lo[e] + rc - 1) // rc for e in range(ne)])

    def onehots(q):
        hits = []
        for e in range(ne):
            slot_row = jnp.concatenate([slot_refs[k][e:e + 1, :] for k in range(cpw)], axis=1)
            hit = (slot_row - window_start(lo[e], q)) == r_iota
            if not isinstance(q, int) or q != 0:
                hit = jnp.logical_and(hit, slot_row >= lo[e] + q * rc)
            hits.append(hit.astype(BF16))
        return jnp.concatenate(hits, axis=0)

    def tn_dot(pc, yy):
        return lax.dot_general(pc, yy, (((0,), (0,)), ((), ())), preferred_element_type=F32)

    pc0 = onehots(0)
    for cp in first_copies(b, c, par):
        cp.wait()
    acc0 = tn_dot(pc0, ybuf_ref[par].reshape(ne * rc, d))

    def extra_pass(q, acc):
        cps = [pltpu.make_async_copy(y_hbm.at[b, e, pl.ds(window_start(lo[e], q), rc)],
                                     yextra_ref.at[e], sem_x.at[e]) for e in range(ne)]
        for cp in cps:
            cp.start()
        pcq = onehots(q)
        for cp in cps:
            cp.wait()
        return acc + tn_dot(pcq, yextra_ref[...].reshape(ne * rc, d))

    acc = lax.fori_loop(1, npass, extra_pass, acc0)
    for k in range(cpw):
        hs3 = hs2_refs[k][...].astype(F32) + acc[k * CHUNK:(k + 1) * CHUNK, :]
        ms = jnp.mean(hs3 * hs3, axis=-1, keepdims=True)
        out_ref[k * CHUNK:(k + 1) * CHUNK, :] = hs3 * lax.rsqrt(ms + EPS) * fg_ref[...]


def kernel(x, meta_tokens, norm1_g, w_in, conv_w, conv_b, lru_w_r, lru_b_r, lru_w_i, lru_b_i,
           lru_lambda, lru_out_g, ret_out_g, w_out, norm2_g, w_router, w_gate, w_up, w_down, final_g):
    bsz, seq, d = x.shape
    depth = norm1_g.shape[0]
    assert depth == 1
    n_meta = meta_tokens.shape[0]
    t_len = seq + n_meta
    pad = (-t_len) % CHUNK
    tp = t_len + pad
    nc = tp // CHUNK
    assert (pad + n_meta) % CHUNK == 0
    c0 = (pad + n_meta) // CHUNK
    rows = bsz * tp
    w_lru = conv_w.shape[-1]
    w_ret = ret_out_g.shape[-1]
    in_cols = w_in.shape[-1]
    cb = w_lru
    assert w_ret == cb and in_cols == 6 * cb and cb % MXU_DIM == 0
    hd = w_ret // RET_HEADS
    ne = w_router.shape[-1]
    ff = w_gate.shape[-1]
    cap = EC_CAPACITY * t_len // ne

    tc = _largest_divisor_leq(nc, 5)
    tm = tc * CHUNK
    n_t = nc // tc
    n_tiles = rows // tm

    assert c0 == 1 and seq % CHUNK == 0
    head = jnp.concatenate([jnp.zeros((pad, d), F32), meta_tokens.astype(F32)], axis=0)
    x4 = x.astype(F32).reshape(bsz, seq // CHUNK, CHUNK, d)

    def x_piece_spec(k):
        return pl.BlockSpec((None, None, CHUNK, d),
                            lambda b, j: (b, jnp.maximum(j * tc + k - 1, 0), 0, 0))

    row_tile_specs = [pl.BlockSpec((CHUNK, d), lambda b, j: (0, 0))] + [x_piece_spec(k) for k in range(tc)]

    half = hd // 2
    freqs = ROPE_BASE ** (-jnp.arange(half, dtype=F32) / half)
    ang = (jnp.arange(tp) - pad).astype(F32)[:, None] * freqs[None, :]
    cos2 = jnp.concatenate([jnp.cos(ang), jnp.cos(ang)], axis=1)
    sin2 = jnp.concatenate([-jnp.sin(ang), jnp.sin(ang)], axis=1)

    w_in_b = w_in[0].astype(BF16)
    w_out_b = w_out[0].astype(BF16)

    z = pl.pallas_call(
        functools.partial(_inproj_kernel, tc=tc, cb=cb, hd=hd, q_scale=float(hd) ** -0.5),
        grid=(bsz, n_t),
        in_specs=row_tile_specs + [
            pl.BlockSpec((1, d), lambda b, j: (0, 0)),
            pl.BlockSpec((d, in_cols), lambda b, j: (0, 0)),
            pl.BlockSpec((tm, hd), lambda b, j: (j, 0)),
            pl.BlockSpec((tm, hd), lambda b, j: (j, 0)),
        ],
        out_specs=pl.BlockSpec((tm, in_cols), lambda b, j: (b * n_t + j, 0)),
        out_shape=jax.ShapeDtypeStruct((rows, in_cols), BF16),
        scratch_shapes=[pltpu.VMEM((tm, d), F32)],
        compiler_params=pltpu.CompilerParams(dimension_semantics=("arbitrary",) * 2,
                                             vmem_limit_bytes=VMEM_LIMIT),
        name="inproj",
    )(head, *([x4] * tc), norm1_g[0][None], w_in_b, cos2, sin2)

    lb = w_lru // LRU_BLOCKS
    per_tile = MXU_DIM // lb
    eye = jnp.eye(per_tile, dtype=F32)

    def blockdiag(wb):
        wb = wb.reshape(2, w_lru // MXU_DIM, per_tile, lb, lb)
        return jnp.einsum('dhpij,pq->dhpiqj', wb, eye).reshape(2, w_lru // MXU_DIM, MXU_DIM, MXU_DIM).astype(BF16)

    wr_bd = blockdiag(0.5 * lru_w_r[0])
    wi_bd = blockdiag(0.5 * lru_w_i[0])
    t_l = max(t for t in range(32, min(tp, 512) + 1, 32) if tp % t == 0 and (t // 32) % 2 == 1)
    n_l = tp // t_l
    seg = t_l // SUBLANES
    assert seg >= CONV_WIDTH and cb % LANES == 0
    nsub = _largest_divisor_leq(n_l, 5)
    t_b = nsub * t_l
    n_lb = n_l // nsub
    hb8 = t_b // BF16_ROWS
    last8 = rows // BF16_ROWS - 1

    def ltile_of(b, p, j):
        return b * n_lb + jnp.where(p == 0, n_lb - 1 - j, j)

    y_lru = pl.pallas_call(
        functools.partial(_lru_kernel, n_t=n_l, t_t=t_l, pad=pad),
        grid=(bsz, 2, n_lb),
        in_specs=[
            pl.BlockSpec((t_b, cb), lambda b, p, j: (ltile_of(b, p, j), 0)),
            pl.BlockSpec((BF16_ROWS, cb), lambda b, p, j: (jnp.maximum(ltile_of(b, p, j) * hb8 - 1, 0), 0)),
            pl.BlockSpec((BF16_ROWS, cb), lambda b, p, j: (jnp.minimum((ltile_of(b, p, j) + 1) * hb8, last8), 0)),
            pl.BlockSpec((t_b, cb), lambda b, p, j: (b * n_lb + p * j, 1)),
            pl.BlockSpec((CONV_WIDTH, cb), lambda b, p, j: (0, 0)),
            pl.BlockSpec((1, cb), lambda b, p, j: (0, 0)),
            pl.BlockSpec((None, w_lru // MXU_DIM, MXU_DIM, MXU_DIM), lambda b, p, j: (1 - p, 0, 0, 0)),
            pl.BlockSpec((None, w_lru // MXU_DIM, MXU_DIM, MXU_DIM), lambda b, p, j: (1 - p, 0, 0, 0)),
            pl.BlockSpec((None, 1, cb), lambda b, p, j: (1 - p, 0, 0)),
            pl.BlockSpec((None, 1, cb), lambda b, p, j: (1 - p, 0, 0)),
            pl.BlockSpec((None, 1, cb), lambda b, p, j: (1 - p, 0, 0)),
            pl.BlockSpec((1, cb), lambda b, p, j: (0, 0)),
        ],
        out_specs=pl.BlockSpec((t_b, cb), lambda b, p, j: (b * n_lb + p * j, 0)),
        out_shape=jax.ShapeDtypeStruct((rows, cb), BF16),
        scratch_shapes=[
            pltpu.VMEM((n_l, seg, SUBLANES, cb), F32),
            pltpu.VMEM((cb // LANES, t_l, LANES), F32),
            pltpu.VMEM((seg + CONV_WIDTH - 1, SUBLANES, cb), F32),
            pltpu.VMEM((seg, SUBLANES, cb), F32),
            pltpu.VMEM((seg, SUBLANES, cb), F32),
            pltpu.VMEM((seg, SUBLANES, cb), F32),
            pltpu.VMEM((seg, SUBLANES, cb), F32),
            pltpu.VMEM((1, cb), F32),
        ],
        compiler_params=pltpu.CompilerParams(dimension_semantics=("arbitrary",) * 3,
                                             vmem_limit_bytes=VMEM_LIMIT),
        name="lru",
    )(z, z, z, z, conv_w[0], conv_b[0][None], wr_bd, wi_bd, 0.5 * lru_b_r[0][:, None], 0.5 * lru_b_i[0][:, None],
      lru_lambda[0][:, None], lru_out_g[0][None])

    log_g = jnp.log(1.0 - jnp.exp2(-5.0 - jnp.arange(RET_HEADS, dtype=F32)))
    idx = jnp.arange(CHUNK, dtype=F32)
    dmat = jnp.exp(log_g[:, None, None] * jnp.abs(idx[:, None] - idx[None, :]))

    def posdec(expo):
        return jnp.broadcast_to(jnp.exp(log_g[:, None] * expo[None, :])[:, :, None], (RET_HEADS, CHUNK, hd))

    dvec = jnp.stack([posdec(idx + 1.0), posdec(CHUNK - idx), posdec(CHUNK - 1.0 - idx), posdec(idx)])
    cdm = jnp.broadcast_to(jnp.exp(log_g * CHUNK)[:, None, None], (RET_HEADS, hd, hd))

    tc_r = _largest_divisor_leq(nc, 13)
    tm_r = tc_r * CHUNK
    n_r = nc // tc_r

    def rtile_of(b, p, j):
        return b * n_r + jnp.where(p == 0, n_r - 1 - j, j)

    y_ret = pl.pallas_call(
        functools.partial(_retention_kernel, n_t=n_r, tc=tc_r, hd=hd),
        grid=(bsz, 2, n_r),
        in_specs=[
            pl.BlockSpec((tm_r, cb), lambda b, p, j: (b * n_r + p * j, 2)),
            pl.BlockSpec((tm_r, cb), lambda b, p, j: (rtile_of(b, p, j), 3)),
            pl.BlockSpec((tm_r, cb), lambda b, p, j: (rtile_of(b, p, j), 4)),
            pl.BlockSpec((tm_r, cb), lambda b, p, j: (b * n_r + p * j, 5)),
            pl.BlockSpec((RET_HEADS, CHUNK, CHUNK), lambda b, p, j: (0, 0, 0)),
            pl.BlockSpec((4, RET_HEADS, CHUNK, hd), lambda b, p, j: (0, 0, 0, 0)),
            pl.BlockSpec((RET_HEADS, hd, hd), lambda b, p, j: (0, 0, 0)),
            pl.BlockSpec((1, cb), lambda b, p, j: (0, 0)),
        ],
        out_specs=pl.BlockSpec((tm_r, cb), lambda b, p, j: (b * n_r + p * j, 0)),
        out_shape=jax.ShapeDtypeStruct((rows, cb), BF16),
        scratch_shapes=[
            pltpu.VMEM((nc, RET_HEADS, hd, hd), BF16),
            pltpu.VMEM((RET_HEADS, hd, hd), F32),
        ],
        compiler_params=pltpu.CompilerParams(dimension_semantics=("arbitrary",) * 3,
                                             vmem_limit_bytes=VMEM_LIMIT),
        name="retention",
    )(z, z, z, z, dmat, dvec, cdm, ret_out_g[0][None])

    assert ne & (ne - 1) == 0 and 3 * ne <= LANES
    dext = d + LANES
    w_router_pad = jnp.pad(w_router[0].astype(F32), ((0, 0), (0, LANES - ne)))
    wr_hi = w_router_pad.astype(BF16)
    wr_mid = (w_router_pad - wr_hi.astype(F32)).astype(BF16)
    hs2, h2e, logits_t = pl.pallas_call(
        functools.partial(_outproj_kernel, tc=tc, ne=ne),
        grid=(bsz, n_t),
        in_specs=[
            pl.BlockSpec((tm, cb), lambda b, j: (b * n_t + j, 0)),
            pl.BlockSpec((tm, cb), lambda b, j: (b * n_t + j, 0)),
        ] + row_tile_specs + [
            pl.BlockSpec((d, d), lambda b, j: (0, 0)),
            pl.BlockSpec((1, d), lambda b, j: (0, 0)),
            pl.BlockSpec((d, 2 * LANES), lambda b, j: (0, 0)),
        ],
        out_specs=[
            pl.BlockSpec((tm, d), lambda b, j: (b * n_t + j, 0)),
            pl.BlockSpec((tm, dext), lambda b, j: (b * n_t + j, 0)),
            pl.BlockSpec((None, ne, tm), lambda b, j: (b, 0, j)),
        ],
        out_shape=[
            jax.ShapeDtypeStruct((rows, d), BF16),
            jax.ShapeDtypeStruct((rows, dext), BF16),
            jax.ShapeDtypeStruct((bsz, ne, tp), F32),
        ],
        scratch_shapes=[pltpu.VMEM((tm, d), F32)],
        compiler_params=pltpu.CompilerParams(dimension_semantics=("arbitrary",) * 2,
                                             vmem_limit_bytes=VMEM_LIMIT),
        name="outproj",
    )(y_lru, y_ret, head, *([x4] * tc), w_out_b, norm2_g[0][None],
      jnp.concatenate([wr_hi, wr_mid], axis=1))

    ncp = _round_up(nc, BF16_ROWS)
    logits4 = jnp.pad(logits_t.reshape(bsz, ne, nc, LANES), ((0, 0), (0, 0), (0, ncp - nc), (0, 0)))
    slot4, off4 = pl.pallas_call(
        functools.partial(_route_kernel, pad=pad, tp=tp, cap=cap),
        grid=(bsz,),
        in_specs=[pl.BlockSpec((None, ne, ncp, LANES), lambda b: (b, 0, 0, 0))],
        out_specs=[pl.BlockSpec((None, ne, ncp, LANES), lambda b: (b, 0, 0, 0))] * 2,
        out_shape=[
            jax.ShapeDtypeStruct((bsz, ne, ncp, LANES), jnp.int32),
            jax.ShapeDtypeStruct((bsz, ne, ncp, LANES), jnp.int32),
        ],
        compiler_params=pltpu.CompilerParams(dimension_semantics=("arbitrary",),
                                             vmem_limit_bytes=VMEM_LIMIT),
        name="route",
    )(logits4)

    start = jnp.concatenate([jnp.transpose(off4[:, :, :nc, 0], (0, 2, 1)),
                             jnp.full((bsz, 1, ne), cap, jnp.int32)], axis=1).reshape(-1)
    nc1 = nc + 1
    slot_rows = slot4[:, :, :nc].reshape(bsz, ne, tp)

    s_rows = _round_up(cap, BF16_ROWS)
    tcd = _largest_divisor_leq(nc, 13)
    w_d = tcd * CHUNK
    n_d = nc // tcd
    rw = (DISPATCH_WINDOW, DISPATCH_WINDOW_PAIR)
    ge = min(ne, SUBLANES)
    assert all(s_rows >= r and (s_rows - r) % BF16_ROWS == 0 for r in rw) and ne % ge == 0
    xs = pl.pallas_call(
        functools.partial(_dispatch_kernel, tcd=tcd, rw=rw, s_rows=s_rows, nc1=nc1, ne=ne),
        grid_spec=pltpu.PrefetchScalarGridSpec(
            num_scalar_prefetch=1,
            grid=(bsz, ne // ge, n_d),
            in_specs=[
                pl.BlockSpec((w_d, dext), lambda b, g, i, st: (b * n_d + i, 0)),
                pl.BlockSpec((None, ge, w_d), lambda b, g, i, st: (b, g, i)),
            ],
            out_specs=pl.BlockSpec((None, ge, s_rows, dext), lambda b, g, i, st: (b, g, 0, 0)),
        ),
        out_shape=jax.ShapeDtypeStruct((bsz, ne, s_rows, dext), BF16),
        compiler_params=pltpu.CompilerParams(dimension_semantics=("arbitrary",) * 3,
                                             vmem_limit_bytes=VMEM_LIMIT),
        name="dispatch",
    )(start, h2e, slot_rows)

    ff_blk = 256 if ff % 256 == 0 else ff
    y_e = pl.pallas_call(
        functools.partial(_ffn_kernel, ne=ne, ff_blk=ff_blk),
        grid=(ne, bsz),
        in_specs=[
            pl.BlockSpec((None, None, s_rows, dext), lambda e, b: (b, e, 0, 0)),
            pl.BlockSpec((None, d, ff), lambda e, b: (e, 0, 0)),
            pl.BlockSpec((None, d, ff), lambda e, b: (e, 0, 0)),
            pl.BlockSpec((None, ff, d), lambda e, b: (e, 0, 0)),
        ],
        out_specs=pl.BlockSpec((None, None, s_rows, d), lambda e, b: (b, e, 0, 0)),
        out_shape=jax.ShapeDtypeStruct((bsz, ne, s_rows, d), BF16),
        scratch_shapes=[
            pltpu.VMEM((d, ff), BF16),
            pltpu.VMEM((d, ff), BF16),
            pltpu.VMEM((ff, d), BF16),
            pltpu.VMEM((s_rows, d), F32),
        ],
        compiler_params=pltpu.CompilerParams(dimension_semantics=("arbitrary",) * 2,
                                             vmem_limit_bytes=VMEM_LIMIT),
        name="ffn",
    )(xs, w_gate[0], w_up[0], w_down[0])

    rc = COMBINE_WINDOW
    cpw = COMBINE_CHUNKS if (nc - c0) % COMBINE_CHUNKS == 0 else 1
    assert (s_rows - rc) % BF16_ROWS == 0
    slot_t = jnp.transpose(slot4[:, :, :nc], (0, 2, 1, 3))
    out = pl.pallas_call(
        functools.partial(_combine_kernel, c0=c0, cpw=cpw, rc=rc, s_rows=s_rows, nc1=nc1),
        grid_spec=pltpu.PrefetchScalarGridSpec(
            num_scalar_prefetch=1,
            grid=(bsz, (nc - c0) // cpw),
            in_specs=(
                [pl.BlockSpec((CHUNK, d), functools.partial(
                    lambda b, c, st, k: (b * nc + c * cpw + c0 + k, 0), k=k)) for k in range(cpw)]
                + [pl.BlockSpec((None, None, ne, LANES), functools.partial(
                    lambda b, c, st, k: (b, c * cpw + c0 + k, 0, 0), k=k)) for k in range(cpw)]
                + [pl.BlockSpec(memory_space=pl.ANY),
                   pl.BlockSpec((1, d), lambda b, c, st: (0, 0))]),
            out_specs=pl.BlockSpec((None, cpw * CHUNK, d), lambda b, c, st: (b, c, 0)),
            scratch_shapes=[
                pltpu.VMEM((2, ne, rc, d), BF16),
                pltpu.VMEM((ne, rc, d), BF16),
                pltpu.SemaphoreType.DMA((2, ne)),
                pltpu.SemaphoreType.DMA((ne,)),
            ],
        ),
        out_shape=jax.ShapeDtypeStruct((bsz, seq, d), x.dtype),
        compiler_params=pltpu.CompilerParams(dimension_semantics=("arbitrary",) * 2,
                                             vmem_limit_bytes=VMEM_LIMIT),
        name="combine",
    )(start, *([hs2] * cpw), *([slot_t] * cpw), y_e, final_g[None])
    return out
```

```python
import functools

import jax
import jax.numpy as jnp
from jax import lax
from jax.experimental import pallas as pl
from jax.experimental.pallas import tpu as pltpu

LRU_BLOCKS = 8
LRU_C = 8.0
CONV_WIDTH = 4
CONV_LEFT = 2
RET_HEADS = 4
CHUNK = 128
ROPE_BASE = 10000.0
EC_CAPACITY = 2
EPS = 1e-6

LANES = 128
SUBLANES = 8
BF16_ROWS = 16
MXU_DIM = 256
V7X_VMEM_BYTES = 64 * 1024 * 1024
VMEM_LIMIT = V7X_VMEM_BYTES * 13 // 16

DISPATCH_WINDOW = 3 * BF16_ROWS
DISPATCH_WINDOW_PAIR = 4 * BF16_ROWS
COMBINE_WINDOW = 4 * BF16_ROWS
COMBINE_CHUNKS = 2

F32 = jnp.float32
BF16 = jnp.bfloat16


def _round_up(x, m):
    return (x + m - 1) // m * m


def _largest_divisor_leq(n, k):
    for d in range(min(n, k), 0, -1):
        if n % d == 0:
            return d
    return 1


def _row_chunk(head_ref, x_refs, k):
    piece = x_refs[k][...]
    if k == 0:
        piece = jnp.where(pl.program_id(1) == 0, head_ref[...], piece)
    return piece


def _inproj_kernel(head_ref, *refs, tc, cb, hd, q_scale):
    x_refs = refs[0:tc]
    g_ref, w_ref, cos_ref, sin_ref, z_ref = refs[tc:]
    hs = []
    for k in range(tc):
        x = _row_chunk(head_ref, x_refs, k)
        ms = jnp.mean(x * x, axis=-1, keepdims=True)
        hs.append((x * lax.rsqrt(ms + EPS) * g_ref[...]).astype(BF16))
    h = jnp.concatenate(hs, axis=0)
    cos = cos_ref[...]
    sin = sin_ref[...]
    for j in (1, 5, 2, 3, 0, 4):
        zj = jnp.dot(h, w_ref[:, j * cb:(j + 1) * cb], preferred_element_type=F32)
        if j in (2, 3):
            for hh in range(cb // hd):
                t = zj[:, hh * hd:(hh + 1) * hd]
                r = pltpu.roll(t, hd // 2, axis=1)
                t = t * cos + r * sin
                if j == 2:
                    t = t * q_scale
                z_ref[:, j * cb + hh * hd:j * cb + (hh + 1) * hd] = t.astype(z_ref.dtype)
        else:
            if j == 1:
                zj = jax.nn.gelu(zj, approximate=True)
            elif j == 5:
                zj = jax.nn.silu(zj)
            z_ref[:, j * cb:(j + 1) * cb] = zj.astype(z_ref.dtype)


def _interleave_in(src_ref, r0, slab_ref, dst_ref, seg, off):
    nsl, t_t = slab_ref.shape[0], slab_ref.shape[1]
    for c in range(nsl):
        slab_ref[c] = src_ref[pl.ds(r0, t_t), c * LANES:(c + 1) * LANES].astype(F32)
    for j in range(seg):
        dst_ref[off + j] = jnp.concatenate(
            [slab_ref[c, pl.ds(j, SUBLANES, stride=seg), :] for c in range(nsl)], axis=1)


def _interleave_out(src_ref, slab_ref, dst_ref, r0, seg):
    nsl, t_t = slab_ref.shape[0], slab_ref.shape[1]
    for j in range(seg):
        row = src_ref[j]
        for c in range(nsl):
            slab_ref[c, pl.ds(j, SUBLANES, stride=seg), :] = row[:, c * LANES:(c + 1) * LANES]
    for c in range(nsl):
        dst_ref[pl.ds(r0, t_t), c * LANES:(c + 1) * LANES] = slab_ref[c].astype(dst_ref.dtype)


def _scan_tile(a_ref, b_ref, h_ref, acc_ref, carry_ref, seg, forward):
    w = a_ref.shape[2]

    unroll = 4 if seg % 4 == 0 else 1

    def body(jb, carry):
        h, acc = carry
        for jo in range(unroll):
            jj = jb * unroll + jo
            j = jj if forward else seg - 1 - jj
            a = a_ref[j]
            h = a * h + b_ref[j]
            acc = a * acc
            h_ref[j] = h
            acc_ref[j] = acc
        return h, acc

    h_end, a_end = lax.fori_loop(0, seg // unroll, body,
                                 (jnp.zeros((SUBLANES, w), F32), jnp.ones((SUBLANES, w), F32)))
    c = carry_ref[...]
    rows = [None] * SUBLANES
    order = range(SUBLANES) if forward else range(SUBLANES - 1, -1, -1)
    for s in order:
        rows[s] = c
        c = a_end[s:s + 1, :] * c + h_end[s:s + 1, :]
    carry_ref[...] = c
    cvec = jnp.concatenate(rows, axis=0)
    return h_ref[...] + acc_ref[...] * cvec[None]


def _lru_kernel(xm_ref, xp_ref, xn_ref, gate_ref, cw_ref, cb_ref, wr_ref, wi_ref, br_ref, bi_ref,
                lam_ref, og_ref, y_ref, hb_ref, slab_ref, xext_ref, a_ref, b_ref, h_ref, acc_ref, carry_ref,
                *, n_t, t_t, pad):
    p = pl.program_id(1)

    @pl.when(pl.program_id(2) == 0)
    def _():
        carry_ref[...] = jnp.zeros_like(carry_ref)

    nsub = xm_ref.shape[0] // t_t

    def sub_tile(k, carry):
        kk = jnp.where(p == 0, nsub - 1 - k, k)
        blk = jnp.where(p == 0, pl.num_programs(2) - 1 - pl.program_id(2), pl.program_id(2))
        _lru_tile(kk, blk * nsub + kk, xm_ref, xp_ref, xn_ref, gate_ref, cw_ref, cb_ref, wr_ref, wi_ref,
                  br_ref, bi_ref, lam_ref, og_ref, y_ref, hb_ref, slab_ref, xext_ref, a_ref, b_ref, h_ref,
                  acc_ref, carry_ref, n_t=n_t, t_t=t_t, pad=pad)
        return carry

    lax.fori_loop(0, nsub, sub_tile, 0)


def _lru_tile(kk, ti, xm_ref, xp_ref, xn_ref, gate_ref, cw_ref, cb_ref, wr_ref, wi_ref, br_ref, bi_ref,
              lam_ref, og_ref, y_ref, hb_ref, slab_ref, xext_ref, a_ref, b_ref, h_ref, acc_ref, carry_ref,
              *, n_t, t_t, pad):
    p = pl.program_id(1)
    w = xm_ref.shape[1]
    seg = t_t // SUBLANES
    nsub = xm_ref.shape[0] // t_t
    nhalo = xp_ref.shape[0]
    r0 = pl.multiple_of(kk * t_t, BF16_ROWS)

    _interleave_in(xm_ref, r0, slab_ref, xext_ref, seg, CONV_LEFT)
    sub = lax.broadcasted_iota(jnp.int32, (SUBLANES, w), 0)
    before = xm_ref[pl.ds(pl.multiple_of(jnp.maximum(r0 - nhalo, 0), BF16_ROWS), nhalo), :]
    after = xm_ref[pl.ds(pl.multiple_of(jnp.minimum(r0 + t_t, (nsub - 1) * t_t), BF16_ROWS), nhalo), :]
    xp8 = jnp.where(kk == 0, xp_ref[...], before).astype(F32)
    xn8 = jnp.where(kk == nsub - 1, xn_ref[...], after).astype(F32)
    xp8 = jnp.where(ti == 0, 0.0, xp8)
    xn8 = jnp.where(ti == n_t - 1, 0.0, xn8)
    for back in range(1, CONV_LEFT + 1):
        xext_ref[CONV_LEFT - back] = jnp.where(
            sub == 0, xp8[nhalo - back:nhalo - back + 1, :],
            pltpu.roll(xext_ref[CONV_LEFT + seg - back], 1, axis=0))
    for fwd in range(CONV_WIDTH - 1 - CONV_LEFT):
        xext_ref[CONV_LEFT + seg + fwd] = jnp.where(
            sub == SUBLANES - 1, xn8[fwd:fwd + 1, :],
            pltpu.roll(xext_ref[CONV_LEFT + fwd], SUBLANES - 1, axis=0))
    u3 = jnp.zeros((seg, SUBLANES, w), F32) + cb_ref[...]
    for k in range(CONV_WIDTH):
        u3 = u3 + cw_ref[k:k + 1, :] * xext_ref[k:k + seg]
    u = u3.reshape(t_t, w)

    ub = u.astype(BF16)
    nh = w // MXU_DIM
    pre_r = jnp.concatenate(
        [jnp.dot(ub[:, q * MXU_DIM:(q + 1) * MXU_DIM], wr_ref[q], preferred_element_type=F32)
         for q in range(nh)], axis=1) + br_ref[...]
    pre_i = jnp.concatenate(
        [jnp.dot(ub[:, q * MXU_DIM:(q + 1) * MXU_DIM], wi_ref[q], preferred_element_type=F32)
         for q in range(nh)], axis=1) + bi_ref[...]
    tr = jnp.tanh(pre_r)
    i = 0.5 + 0.5 * jnp.tanh(pre_i)
    nl = -lam_ref[...]
    en = jnp.exp(-jnp.abs(nl))
    w1 = 1.0 + en
    log1p_en = jnp.where(w1 == 1.0, en, en * jnp.log(w1) / jnp.where(w1 == 1.0, 1.0, w1 - 1.0))
    half_c = (-0.5 * LRU_C) * (jnp.maximum(nl, 0.0) + log1p_en)
    log_a = half_c + half_c * tr
    a = jnp.exp(log_a)
    v = -jnp.tanh(log_a) * (1.0 + a * a)
    gx = jnp.where(v > 0.0, v * lax.rsqrt(v), 0.0) * (i * u)
    a_ref[...] = a.reshape(seg, SUBLANES, w)
    b_ref[...] = gx.reshape(seg, SUBLANES, w)

    @pl.when(jnp.logical_and(p == 1, ti * t_t < pad))
    def _():
        j3 = lax.broadcasted_iota(jnp.int32, (seg, SUBLANES, 1), 0)
        s3 = lax.broadcasted_iota(jnp.int32, (seg, SUBLANES, 1), 1)
        rowidx = ti * t_t + s3 * seg + j3
        b_ref[...] = jnp.where(rowidx >= pad, b_ref[...], 0.0)

    @pl.when(p == 0)
    def _():
        hb_ref[ti] = _scan_tile(a_ref, b_ref, h_ref, acc_ref, carry_ref, seg, False)

    @pl.when(p == 1)
    def _():
        h = _scan_tile(a_ref, b_ref, h_ref, acc_ref, carry_ref, seg, True) + hb_ref[ti]
        _interleave_in(gate_ref, r0, slab_ref, b_ref, seg, 0)
        y = h * b_ref[...]
        ms = jnp.mean(y * y, axis=-1, keepdims=True)
        h_ref[...] = y * lax.rsqrt(ms + EPS) * og_ref[...]
        _interleave_out(h_ref, slab_ref, y_ref, r0, seg)


def _retention_kernel(q_ref, k_ref, v_ref, g_ref, dmat_ref, dvec_ref, cd_ref, og_ref, y_ref,
                      sb_ref, s_ref, *, n_t, tc, hd):
    p = pl.program_id(1)
    j = pl.program_id(2)
    ti = jnp.where(p == 0, n_t - 1 - j, j)
    nheads = q_ref.shape[1] // hd

    @pl.when(j == 0)
    def _():
        s_ref[...] = jnp.zeros_like(s_ref)

    def tn_dot(x, y):
        return lax.dot_general(x, y, (((0,), (0,)), ((), ())), preferred_element_type=F32)

    def nt_dot(x, y):
        return lax.dot_general(x, y, (((1,), (1,)), ((), ())), preferred_element_type=F32)

    @pl.when(p == 0)
    def _():
        for c in reversed(range(tc)):
            cg = ti * tc + c
            for h in range(nheads):
                rs = slice(c * CHUNK, (c + 1) * CHUNK)
                cs = slice(h * hd, (h + 1) * hd)
                kk = k_ref[rs, cs].astype(F32)
                vv = v_ref[rs, cs].astype(BF16)
                sb_ref[cg, h] = s_ref[h].astype(BF16)
                s_ref[h] = s_ref[h] * cd_ref[h] + tn_dot((kk * dvec_ref[3, h]).astype(BF16), vv)

    @pl.when(p == 1)
    def _():
        for c in range(tc):
            cg = ti * tc + c
            for h in range(nheads):
                rs = slice(c * CHUNK, (c + 1) * CHUNK)
                cs = slice(h * hd, (h + 1) * hd)
                qq = q_ref[rs, cs].astype(F32)
                kk = k_ref[rs, cs].astype(F32)
                vv = v_ref[rs, cs].astype(BF16)
                sc = nt_dot(qq.astype(BF16), kk.astype(BF16)) * dmat_ref[h]
                o = jnp.dot(sc.astype(BF16), vv, preferred_element_type=F32)
                o = o + jnp.dot((qq * dvec_ref[0, h]).astype(BF16), s_ref[h].astype(BF16),
                                preferred_element_type=F32)
                o = o + jnp.dot((qq * dvec_ref[1, h]).astype(BF16), sb_ref[cg, h],
                                preferred_element_type=F32)
                s_ref[h] = s_ref[h] * cd_ref[h] + tn_dot((kk * dvec_ref[2, h]).astype(BF16), vv)
                ms = jnp.mean(o * o, axis=-1, keepdims=True)
                o = o * lax.rsqrt(ms + EPS) * og_ref[:, cs]
                y_ref[rs, cs] = (o * g_ref[rs, cs].astype(F32)).astype(y_ref.dtype)


def _outproj_kernel(yl_ref, yr_ref, head_ref, *refs, tc, ne):
    x_refs = refs[0:tc]
    w_ref, g_ref, wr2_ref, hs2_ref, h2e_ref, lg_ref = refs[tc:]
    d = hs2_ref.shape[1]
    kcut = (tc + 1) // 2
    for ks in ([range(0, kcut), range(kcut, tc)] if kcut < tc else [range(0, tc)]):
        rs = slice(ks[0] * CHUNK, (ks[-1] + 1) * CHUNK)
        xt = jnp.concatenate([_row_chunk(head_ref, x_refs, k) for k in ks], axis=0)
        ymix = jnp.concatenate([yl_ref[rs, :].astype(BF16), yr_ref[rs, :].astype(BF16)], axis=1)
        mix = jnp.dot(ymix, w_ref[...], preferred_element_type=F32)
        hs2 = xt + mix
        hs2_ref[rs, :] = hs2.astype(hs2_ref.dtype)
        ms = jnp.mean(hs2 * hs2, axis=-1, keepdims=True)
        h2 = hs2 * lax.rsqrt(ms + EPS) * g_ref[...]
        hh = h2.astype(BF16)
        h2e_ref[rs, 0:d] = hh
        hm = (h2 - hh.astype(F32)).astype(BF16)
        w2 = wr2_ref[...]
        d_hi = jnp.dot(hh, w2, preferred_element_type=F32)
        d_mid = jnp.dot(hm, w2, preferred_element_type=F32)
        lr = d_hi[:, 0:LANES] + (d_hi[:, LANES:] + d_mid[:, 0:LANES])
        lg_ref[:, rs] = lr.T[0:ne, :]

        lane = lax.broadcasted_iota(jnp.int32, lr.shape, 1)
        live = lane < ne
        mx = jnp.max(jnp.where(live, lr, -jnp.inf), axis=1, keepdims=True)
        ex = jnp.where(live, jnp.exp(lr - mx), 0.0)
        aff = ex / jnp.sum(ex, axis=1, keepdims=True)
        a_hi = aff.astype(BF16).astype(F32)
        r1 = aff - a_hi
        a_mid = r1.astype(BF16).astype(F32)
        a_lo = (r1 - a_mid).astype(BF16).astype(F32)
        ext = a_hi + pltpu.roll(a_mid, ne, axis=1) + pltpu.roll(a_lo, 2 * ne, axis=1)
        h2e_ref[rs, d:] = ext.astype(BF16)


def _route_kernel(lg_ref, slot_ref, off_ref, *, pad, tp, cap):
    ne, nc, _ = lg_ref.shape
    lg = lg_ref[...]
    m = jnp.max(lg, axis=0, keepdims=True)
    ex = jnp.exp(lg - m)
    aff = ex / jnp.sum(ex, axis=0, keepdims=True)
    tpos = (lax.broadcasted_iota(jnp.int32, (nc, LANES), 0) * LANES
            + lax.broadcasted_iota(jnp.int32, (nc, LANES), 1))
    valid = jnp.logical_and(tpos >= pad, tpos < tp)[None]
    affm = jnp.where(valid, aff, -1.0)

    def count(mask):
        c = jnp.sum(jnp.where(mask, 1.0, 0.0), axis=1, keepdims=True)
        return jnp.sum(c, axis=2, keepdims=True)

    def as_float(bits):
        return lax.bitcast_convert_type(bits, F32)

    def bs_body(_, lohi):
        lo, hi = lohi
        mid = lo + lax.shift_right_logical(hi - lo, 1)
        ok = count(affm >= as_float(mid)) >= float(cap)
        return jnp.where(ok, mid, lo), jnp.where(ok, hi, mid)

    lo0 = jnp.zeros((ne, 1, 1), jnp.int32)
    hi0 = jnp.full((ne, 1, 1), 0x3F800001, jnp.int32)
    thr_bits, _ = lax.fori_loop(0, 31, bs_body, (lo0, hi0))
    thr = as_float(thr_bits)

    gt = affm > thr
    eq = affm == thr
    need = float(cap) - count(gt)

    ci = lax.broadcasted_iota(jnp.int32, (LANES, LANES), 0)
    cj = lax.broadcasted_iota(jnp.int32, (LANES, LANES), 1)
    tri = (ci <= cj).astype(BF16)
    ri = lax.broadcasted_iota(jnp.int32, (nc, nc), 0)
    rj = lax.broadcasted_iota(jnp.int32, (nc, nc), 1)
    ltri = (rj < ri).astype(BF16)

    def prefix(mask2d):
        within = jnp.dot(mask2d.astype(BF16), tri, preferred_element_type=F32)
        tot = jnp.broadcast_to(within[:, LANES - 1:LANES], (nc, LANES)).astype(BF16)
        off = jnp.dot(ltri, tot, preferred_element_type=F32)
        return within + off, off

    for e in range(ne):
        tie_incl, _ = prefix(eq[e])
        tie_excl = tie_incl - eq[e].astype(F32)
        sel = jnp.logical_or(gt[e], jnp.logical_and(eq[e], tie_excl < need[e]))
        cum, off = prefix(sel)
        slot_ref[e] = jnp.where(sel, cum.astype(jnp.int32) - 1, -1)
        off_ref[e] = off.astype(jnp.int32)


def _dispatch_kernel(start_ref, h2e_ref, slot_ref, xs_ref, *, tcd, rw, s_rows, nc1, ne):
    b = pl.program_id(0)
    g = pl.program_id(1)
    i = pl.program_id(2)
    ge = slot_ref.shape[0]
    dext = h2e_ref.shape[1]

    @pl.when(i == 0)
    def _():
        xs_ref[...] = jnp.zeros_like(xs_ref)

    rw1, rw2 = rw
    for c in range(0, tcd, 2):
        gc = min(2, tcd - c)
        rw = rw2 if gc == 2 else rw1
        r_iota = lax.broadcasted_iota(jnp.int32, (rw, 1), 0)
        base = (b * nc1 + i * tcd + c) * ne + g * ge
        s1 = [start_ref[base + gc * ne + e] for e in range(ge)]
        lo = [(start_ref[base + e] // BF16_ROWS) * BF16_ROWS for e in range(ge)]
        npass = functools.reduce(jnp.maximum, [(s1[e] - lo[e] + rw - 1) // rw for e in range(ge)])
        rows = slice(c * CHUNK, (c + gc) * CHUNK)

        def one_pass(q, lo=lo, rows=rows, rw=rw, r_iota=r_iota):
            ws = [pl.multiple_of(jnp.minimum(lo[e] + q * rw, s_rows - rw), BF16_ROWS) for e in range(ge)]
            hits = []
            for e in range(ge):
                slot_row = slot_ref[e:e + 1, rows]
                hit = (slot_row - ws[e]) == r_iota
                if not isinstance(q, int) or q != 0:
                    hit = jnp.logical_and(hit, slot_row >= lo[e] + q * rw)
                hits.append(hit.astype(BF16))
            pc = jnp.concatenate(hits, axis=0)
            for n in range(0, dext, MXU_DIM):
                cs = slice(n, min(n + MXU_DIM, dext))
                res = jnp.dot(pc, h2e_ref[rows, cs], preferred_element_type=F32)
                for e in range(ge):
                    xs_ref[e, pl.ds(ws[e], rw), cs] += res[e * rw:(e + 1) * rw].astype(BF16)

        one_pass(0)

        def extra_pass(q, carry, one_pass=one_pass):
            one_pass(q)
            return carry

        lax.fori_loop(1, npass, extra_pass, 0)


def _ffn_kernel(xs_ref, wg_ref, wu_ref, wd_ref, y_ref, wgb_ref, wub_ref, wdb_ref, yacc_ref, *, ne, ff_blk):
    e = pl.program_id(0)

    @pl.when(pl.program_id(1) == 0)
    def _():
        wgb_ref[...] = wg_ref[...].astype(BF16)
        wub_ref[...] = wu_ref[...].astype(BF16)
        wdb_ref[...] = wd_ref[...].astype(BF16)

    wg_ref, wu_ref, wd_ref = wgb_ref, wub_ref, wdb_ref
    d = wg_ref.shape[0]
    xs = xs_ref[:, 0:d]
    ext = xs_ref[:, d:].astype(F32)
    lane = lax.broadcasted_iota(jnp.int32, ext.shape, 1)
    gate = jnp.sum(jnp.where(jnp.bitwise_and(lane, ne - 1) == e, ext, 0.0), axis=1, keepdims=True)
    ff = wg_ref.shape[1]
    for f in range(ff // ff_blk):
        fs = slice(f * ff_blk, (f + 1) * ff_blk)
        a = jnp.dot(xs, wg_ref[:, fs], preferred_element_type=F32)
        u = jnp.dot(xs, wu_ref[:, fs], preferred_element_type=F32)
        act = (jax.nn.silu(a) * u).astype(BF16)
        contrib = jnp.dot(act, wd_ref[fs, :], preferred_element_type=F32)
        if f == 0:
            yacc_ref[...] = contrib
        else:
            yacc_ref[...] += contrib
    y_ref[...] = (yacc_ref[...] * gate).astype(BF16)


def _combine_kernel(start_ref, *refs, c0, cpw, rc, s_rows, nc1):
    hs2_refs = refs[0:cpw]
    slot_refs = refs[cpw:2 * cpw]
    y_hbm, fg_ref, out_ref, ybuf_ref, yextra_ref, sem, sem_x = refs[2 * cpw:]
    b = pl.program_id(0)
    cc = pl.program_id(1)
    n_c = pl.num_programs(1)
    ne = slot_refs[0].shape[0]
    d = ybuf_ref.shape[3]
    n = b * n_c + cc
    par = n % 2
    r_iota = lax.broadcasted_iota(jnp.int32, (rc, 1), 0)

    def window_lo(bb, chunk):
        base = (bb * nc1 + chunk) * ne
        return [(start_ref[base + e] // BF16_ROWS) * BF16_ROWS for e in range(ne)]

    def window_start(lo_e, q):
        return pl.multiple_of(jnp.minimum(lo_e + q * rc, s_rows - rc), BF16_ROWS)

    def first_copies(bb, chunk, buf):
        lo_ = window_lo(bb, chunk)
        return [pltpu.make_async_copy(y_hbm.at[bb, e, pl.ds(window_start(lo_[e], 0), rc)],
                                      ybuf_ref.at[buf, e], sem.at[buf, e]) for e in range(ne)]

    @pl.when(n == 0)
    def _():
        for cp in first_copies(b, cc * cpw + c0, par):
            cp.start()

    @pl.when(n + 1 < pl.num_programs(0) * n_c)
    def _():
        n1 = n + 1
        for cp in first_copies(n1 // n_c, (n1 % n_c) * cpw + c0, 1 - par):
            cp.start()

    c = cc * cpw + c0
    base = (b * nc1 + c) * ne
    s1 = [start_ref[base + cpw * ne + e] for e in range(ne)]
    lo = window_lo(b, c)
    npass = functools.reduce(jnp.maximum, [(s1[e] - lo[e] + rc - 1) // rc for e in range(ne)])

    def onehots(q):
        hits = []
        for e in range(ne):
            slot_row = jnp.concatenate([slot_refs[k][e:e + 1, :] for k in range(cpw)], axis=1)
            hit = (slot_row - window_start(lo[e], q)) == r_iota
            if not isinstance(q, int) or q != 0:
                hit = jnp.logical_and(hit, slot_row >= lo[e] + q * rc)
            hits.append(hit.astype(BF16))
        return jnp.concatenate(hits, axis=0)

    def tn_dot(pc, yy):
        return lax.dot_general(pc, yy, (((0,), (0,)), ((), ())), preferred_element_type=F32)

    pc0 = onehots(0)
    for cp in first_copies(b, c, par):
        cp.wait()
    acc0 = tn_dot(pc0, ybuf_ref[par].reshape(ne * rc, d))

    def extra_pass(q, acc):
        cps = [pltpu.make_async_copy(y_hbm.at[b, e, pl.ds(window_start(lo[e], q), rc)],
                                     yextra_ref.at[e], sem_x.at[e]) for e in range(ne)]
        for cp in cps:
            cp.start()
        pcq = onehots(q)
        for cp in cps:
            cp.wait()
        return acc + tn_dot(pcq, yextra_ref[...].reshape(ne * rc, d))

    acc = lax.fori_loop(1, npass, extra_pass, acc0)
    for k in range(cpw):
        hs3 = hs2_refs[k][...].astype(F32) + acc[k * CHUNK:(k + 1) * CHUNK, :]
        ms = jnp.mean(hs3 * hs3, axis=-1, keepdims=True)
        out_ref[k * CHUNK:(k + 1) * CHUNK, :] = hs3 * lax.rsqrt(ms + EPS) * fg_ref[...]


def kernel(x, meta_tokens, norm1_g, w_in, conv_w, conv_b, lru_w_r, lru_b_r, lru_w_i, lru_b_i,
           lru_lambda, lru_out_g, ret_out_g, w_out, norm2_g, w_router, w_gate, w_up, w_down, final_g):
    bsz, seq, d = x.shape
    depth = norm1_g.shape[0]
    assert depth == 1
    n_meta = meta_tokens.shape[0]
    t_len = seq + n_meta
    pad = (-t_len) % CHUNK
    tp = t_len + pad
    nc = tp // CHUNK
    assert (pad + n_meta) % CHUNK == 0
    c0 = (pad + n_meta) // CHUNK
    rows = bsz * tp
    w_lru = conv_w.shape[-1]
    w_ret = ret_out_g.shape[-1]
    in_cols = w_in.shape[-1]
    cb = w_lru
    assert w_ret == cb and in_cols == 6 * cb and cb % MXU_DIM == 0
    hd = w_ret // RET_HEADS
    ne = w_router.shape[-1]
    ff = w_gate.shape[-1]
    cap = EC_CAPACITY * t_len // ne

    tc = _largest_divisor_leq(nc, 5)
    tm = tc * CHUNK
    n_t = nc // tc
    n_tiles = rows // tm

    assert c0 == 1 and seq % CHUNK == 0
    head = jnp.concatenate([jnp.zeros((pad, d), F32), meta_tokens.astype(F32)], axis=0)
    x4 = x.astype(F32).reshape(bsz, seq // CHUNK, CHUNK, d)

    def x_piece_spec(k):
        return pl.BlockSpec((None, None, CHUNK, d),
                            lambda b, j: (b, jnp.maximum(j * tc + k - 1, 0), 0, 0))

    row_tile_specs = [pl.BlockSpec((CHUNK, d), lambda b, j: (0, 0))] + [x_piece_spec(k) for k in range(tc)]

    half = hd // 2
    freqs = ROPE_BASE ** (-jnp.arange(half, dtype=F32) / half)
    ang = (jnp.arange(tp) - pad).astype(F32)[:, None] * freqs[None, :]
    cos2 = jnp.concatenate([jnp.cos(ang), jnp.cos(ang)], axis=1)
    sin2 = jnp.concatenate([-jnp.sin(ang), jnp.sin(ang)], axis=1)

    w_in_b = w_in[0].astype(BF16)
    w_out_b = w_out[0].astype(BF16)

    z = pl.pallas_call(
        functools.partial(_inproj_kernel, tc=tc, cb=cb, hd=hd, q_scale=float(hd) ** -0.5),
        grid=(bsz, n_t),
        in_specs=row_tile_specs + [
            pl.BlockSpec((1, d), lambda b, j: (0, 0)),
            pl.BlockSpec((d, in_cols), lambda b, j: (0, 0)),
            pl.BlockSpec((tm, hd), lambda b, j: (j, 0)),
            pl.BlockSpec((tm, hd), lambda b, j: (j, 0)),
        ],
        out_specs=pl.BlockSpec((tm, in_cols), lambda b, j: (b * n_t + j, 0)),
        out_shape=jax.ShapeDtypeStruct((rows, in_cols), BF16),
        compiler_params=pltpu.CompilerParams(dimension_semantics=("arbitrary",) * 2,
                                             vmem_limit_bytes=VMEM_LIMIT),
        name="inproj",
    )(head, *([x4] * tc), norm1_g[0][None], w_in_b, cos2, sin2)

    lb = w_lru // LRU_BLOCKS
    per_tile = MXU_DIM // lb
    eye = jnp.eye(per_tile, dtype=F32)

    def blockdiag(wb):
        wb = wb.reshape(2, w_lru // MXU_DIM, per_tile, lb, lb)
        return jnp.einsum('dhpij,pq->dhpiqj', wb, eye).reshape(2, w_lru // MXU_DIM, MXU_DIM, MXU_DIM).astype(BF16)

    wr_bd = blockdiag(0.5 * lru_w_r[0])
    wi_bd = blockdiag(0.5 * lru_w_i[0])
    t_l = max(t for t in range(32, min(tp, 512) + 1, 32) if tp % t == 0 and (t // 32) % 2 == 1)
    n_l = tp // t_l
    seg = t_l // SUBLANES
    assert seg >= CONV_WIDTH and cb % LANES == 0
    nsub = _largest_divisor_leq(n_l, 5)
    t_b = nsub * t_l
    n_lb = n_l // nsub
    hb8 = t_b // BF16_ROWS
    last8 = rows // BF16_ROWS - 1

    def ltile_of(b, p, j):
        return b * n_lb + jnp.where(p == 0, n_lb - 1 - j, j)

    y_lru = pl.pallas_call(
        functools.partial(_lru_kernel, n_t=n_l, t_t=t_l, pad=pad),
        grid=(bsz, 2, n_lb),
        in_specs=[
            pl.BlockSpec((t_b, cb), lambda b, p, j: (ltile_of(b, p, j), 0)),
            pl.BlockSpec((BF16_ROWS, cb), lambda b, p, j: (jnp.maximum(ltile_of(b, p, j) * hb8 - 1, 0), 0)),
            pl.BlockSpec((BF16_ROWS, cb), lambda b, p, j: (jnp.minimum((ltile_of(b, p, j) + 1) * hb8, last8), 0)),
            pl.BlockSpec((t_b, cb), lambda b, p, j: (b * n_lb + p * j, 1)),
            pl.BlockSpec((CONV_WIDTH, cb), lambda b, p, j: (0, 0)),
            pl.BlockSpec((1, cb), lambda b, p, j: (0, 0)),
            pl.BlockSpec((None, w_lru // MXU_DIM, MXU_DIM, MXU_DIM), lambda b, p, j: (1 - p, 0, 0, 0)),
            pl.BlockSpec((None, w_lru // MXU_DIM, MXU_DIM, MXU_DIM), lambda b, p, j: (1 - p, 0, 0, 0)),
            pl.BlockSpec((None, 1, cb), lambda b, p, j: (1 - p, 0, 0)),
            pl.BlockSpec((None, 1, cb), lambda b, p, j: (1 - p, 0, 0)),
            pl.BlockSpec((None, 1, cb), lambda b, p, j: (1 - p, 0, 0)),
            pl.BlockSpec((1, cb), lambda b, p, j: (0, 0)),
        ],
        out_specs=pl.BlockSpec((t_b, cb), lambda b, p, j: (b * n_lb + p * j, 0)),
        out_shape=jax.ShapeDtypeStruct((rows, cb), BF16),
        scratch_shapes=[
            pltpu.VMEM((n_l, seg, SUBLANES, cb), F32),
            pltpu.VMEM((cb // LANES, t_l, LANES), F32),
            pltpu.VMEM((seg + CONV_WIDTH - 1, SUBLANES, cb), F32),
            pltpu.VMEM((seg, SUBLANES, cb), F32),
            pltpu.VMEM((seg, SUBLANES, cb), F32),
            pltpu.VMEM((seg, SUBLANES, cb), F32),
            pltpu.VMEM((seg, SUBLANES, cb), F32),
            pltpu.VMEM((1, cb), F32),
        ],
        compiler_params=pltpu.CompilerParams(dimension_semantics=("arbitrary",) * 3,
                                             vmem_limit_bytes=VMEM_LIMIT),
        name="lru",
    )(z, z, z, z, conv_w[0], conv_b[0][None], wr_bd, wi_bd, 0.5 * lru_b_r[0][:, None], 0.5 * lru_b_i[0][:, None],
      lru_lambda[0][:, None], lru_out_g[0][None])

    log_g = jnp.log(1.0 - jnp.exp2(-5.0 - jnp.arange(RET_HEADS, dtype=F32)))
    idx = jnp.arange(CHUNK, dtype=F32)
    dmat = jnp.exp(log_g[:, None, None] * jnp.abs(idx[:, None] - idx[None, :]))

    def posdec(expo):
        return jnp.broadcast_to(jnp.exp(log_g[:, None] * expo[None, :])[:, :, None], (RET_HEADS, CHUNK, hd))

    dvec = jnp.stack([posdec(idx + 1.0), posdec(CHUNK - idx), posdec(CHUNK - 1.0 - idx), posdec(idx)])
    cdm = jnp.broadcast_to(jnp.exp(log_g * CHUNK)[:, None, None], (RET_HEADS, hd, hd))

    tc_r = _largest_divisor_leq(nc, 13)
    tm_r = tc_r * CHUNK
    n_r = nc // tc_r

    def rtile_of(b, p, j):
        return b * n_r + jnp.where(p == 0, n_r - 1 - j, j)

    y_ret = pl.pallas_call(
        functools.partial(_retention_kernel, n_t=n_r, tc=tc_r, hd=hd),
        grid=(bsz, 2, n_r),
        in_specs=[
            pl.BlockSpec((tm_r, cb), lambda b, p, j: (b * n_r + p * j, 2)),
            pl.BlockSpec((tm_r, cb), lambda b, p, j: (rtile_of(b, p, j), 3)),
            pl.BlockSpec((tm_r, cb), lambda b, p, j: (rtile_of(b, p, j), 4)),
            pl.BlockSpec((tm_r, cb), lambda b, p, j: (b * n_r + p * j, 5)),
            pl.BlockSpec((RET_HEADS, CHUNK, CHUNK), lambda b, p, j: (0, 0, 0)),
            pl.BlockSpec((4, RET_HEADS, CHUNK, hd), lambda b, p, j: (0, 0, 0, 0)),
            pl.BlockSpec((RET_HEADS, hd, hd), lambda b, p, j: (0, 0, 0)),
            pl.BlockSpec((1, cb), lambda b, p, j: (0, 0)),
        ],
        out_specs=pl.BlockSpec((tm_r, cb), lambda b, p, j: (b * n_r + p * j, 0)),
        out_shape=jax.ShapeDtypeStruct((rows, cb), BF16),
        scratch_shapes=[
            pltpu.VMEM((nc, RET_HEADS, hd, hd), BF16),
            pltpu.VMEM((RET_HEADS, hd, hd), F32),
        ],
        compiler_params=pltpu.CompilerParams(dimension_semantics=("arbitrary",) * 3,
                                             vmem_limit_bytes=VMEM_LIMIT),
        name="retention",
    )(z, z, z, z, dmat, dvec, cdm, ret_out_g[0][None])

    assert ne & (ne - 1) == 0 and 3 * ne <= LANES
    dext = d + LANES
    w_router_pad = jnp.pad(w_router[0].astype(F32), ((0, 0), (0, LANES - ne)))
    wr_hi = w_router_pad.astype(BF16)
    wr_mid = (w_router_pad - wr_hi.astype(F32)).astype(BF16)
    hs2, h2e, logits_t = pl.pallas_call(
        functools.partial(_outproj_kernel, tc=tc, ne=ne),
        grid=(bsz, n_t),
        in_specs=[
            pl.BlockSpec((tm, cb), lambda b, j: (b * n_t + j, 0)),
            pl.BlockSpec((tm, cb), lambda b, j: (b * n_t + j, 0)),
        ] + row_tile_specs + [
            pl.BlockSpec((d, d), lambda b, j: (0, 0)),
            pl.BlockSpec((1, d), lambda b, j: (0, 0)),
            pl.BlockSpec((d, 2 * LANES), lambda b, j: (0, 0)),
        ],
        out_specs=[
            pl.BlockSpec((tm, d), lambda b, j: (b * n_t + j, 0)),
            pl.BlockSpec((tm, dext), lambda b, j: (b * n_t + j, 0)),
            pl.BlockSpec((None, ne, tm), lambda b, j: (b, 0, j)),
        ],
        out_shape=[
            jax.ShapeDtypeStruct((rows, d), BF16),
            jax.ShapeDtypeStruct((rows, dext), BF16),
            jax.ShapeDtypeStruct((bsz, ne, tp), F32),
        ],
        compiler_params=pltpu.CompilerParams(dimension_semantics=("arbitrary",) * 2,
                                             vmem_limit_bytes=VMEM_LIMIT),
        name="outproj",
    )(y_lru, y_ret, head, *([x4] * tc), w_out_b, norm2_g[0][None],
      jnp.concatenate([wr_hi, wr_mid], axis=1))

    ncp = _round_up(nc, BF16_ROWS)
    logits4 = jnp.pad(logits_t.reshape(bsz, ne, nc, LANES), ((0, 0), (0, 0), (0, ncp - nc), (0, 0)))
    slot4, off4 = pl.pallas_call(
        functools.partial(_route_kernel, pad=pad, tp=tp, cap=cap),
        grid=(bsz,),
        in_specs=[pl.BlockSpec((None, ne, ncp, LANES), lambda b: (b, 0, 0, 0))],
        out_specs=[pl.BlockSpec((None, ne, ncp, LANES), lambda b: (b, 0, 0, 0))] * 2,
        out_shape=[
            jax.ShapeDtypeStruct((bsz, ne, ncp, LANES), jnp.int32),
            jax.ShapeDtypeStruct((bsz, ne, ncp, LANES), jnp.int32),
        ],
        compiler_params=pltpu.CompilerParams(dimension_semantics=("arbitrary",),
                                             vmem_limit_bytes=VMEM_LIMIT),
        name="route",
    )(logits4)

    start = jnp.concatenate([jnp.transpose(off4[:, :, :nc, 0], (0, 2, 1)),
                             jnp.full((bsz, 1, ne), cap, jnp.int32)], axis=1).reshape(-1)
    nc1 = nc + 1
    slot_rows = slot4[:, :, :nc].reshape(bsz, ne, tp)

    s_rows = _round_up(cap, BF16_ROWS)
    tcd = _largest_divisor_leq(nc, 13)
    w_d = tcd * CHUNK
    n_d = nc // tcd
    rw = (DISPATCH_WINDOW, DISPATCH_WINDOW_PAIR)
    ge = min(ne, SUBLANES)
    assert all(s_rows >= r and (s_rows - r) % BF16_ROWS == 0 for r in rw) and ne % ge == 0
    xs = pl.pallas_call(
        functools.partial(_dispatch_kernel, tcd=tcd, rw=rw, s_rows=s_rows, nc1=nc1, ne=ne),
        grid_spec=pltpu.PrefetchScalarGridSpec(
            num_scalar_prefetch=1,
            grid=(bsz, ne // ge, n_d),
            in_specs=[
                pl.BlockSpec((w_d, dext), lambda b, g, i, st: (b * n_d + i, 0)),
                pl.BlockSpec((None, ge, w_d), lambda b, g, i, st: (b, g, i)),
            ],
            out_specs=pl.BlockSpec((None, ge, s_rows, dext), lambda b, g, i, st: (b, g, 0, 0)),
        ),
        out_shape=jax.ShapeDtypeStruct((bsz, ne, s_rows, dext), BF16),
        compiler_params=pltpu.CompilerParams(dimension_semantics=("arbitrary",) * 3,
                                             vmem_limit_bytes=VMEM_LIMIT),
        name="dispatch",
    )(start, h2e, slot_rows)

    ff_blk = 256 if ff % 256 == 0 else ff
    y_e = pl.pallas_call(
        functools.partial(_ffn_kernel, ne=ne, ff_blk=ff_blk),
        grid=(ne, bsz),
        in_specs=[
            pl.BlockSpec((None, None, s_rows, dext), lambda e, b: (b, e, 0, 0)),
            pl.BlockSpec((None, d, ff), lambda e, b: (e, 0, 0)),
            pl.BlockSpec((None, d, ff), lambda e, b: (e, 0, 0)),
            pl.BlockSpec((None, ff, d), lambda e, b: (e, 0, 0)),
        ],
        out_specs=pl.BlockSpec((None, None, s_rows, d), lambda e, b: (b, e, 0, 0)),
        out_shape=jax.ShapeDtypeStruct((bsz, ne, s_rows, d), BF16),
        scratch_shapes=[
            pltpu.VMEM((d, ff), BF16),
            pltpu.VMEM((d, ff), BF16),
            pltpu.VMEM((ff, d), BF16),
            pltpu.VMEM((s_rows, d), F32),
        ],
        compiler_params=pltpu.CompilerParams(dimension_semantics=("arbitrary",) * 2,
                                             vmem_limit_bytes=VMEM_LIMIT),
        name="ffn",
    )(xs, w_gate[0], w_up[0], w_down[0])

    rc = COMBINE_WINDOW
    cpw = COMBINE_CHUNKS if (nc - c0) % COMBINE_CHUNKS == 0 else 1
    assert (s_rows - rc) % BF16_ROWS == 0
    slot_t = jnp.transpose(slot4[:, :, :nc], (0, 2, 1, 3))
    out = pl.pallas_call(
        functools.partial(_combine_kernel, c0=c0, cpw=cpw, rc=rc, s_rows=s_rows, nc1=nc1),
        grid_spec=pltpu.PrefetchScalarGridSpec(
            num_scalar_prefetch=1,
            grid=(bsz, (nc - c0) // cpw),
            in_specs=(
                [pl.BlockSpec((CHUNK, d), functools.partial(
                    lambda b, c, st, k: (b * nc + c * cpw + c0 + k, 0), k=k)) for k in range(cpw)]
                + [pl.BlockSpec((None, None, ne, LANES), functools.partial(
                    lambda b, c, st, k: (b, c * cpw + c0 + k, 0, 0), k=k)) for k in range(cpw)]
                + [pl.BlockSpec(memory_space=pl.ANY),
                   pl.BlockSpec((1, d), lambda b, c, st: (0, 0))]),
            out_specs=pl.BlockSpec((None, cpw * CHUNK, d), lambda b, c, st: (b, c, 0)),
            scratch_shapes=[
                pltpu.VMEM((2, ne, rc, d), BF16),
                pltpu.VMEM((ne, rc, d), BF16),
                pltpu.SemaphoreType.DMA((2, ne)),
                pltpu.SemaphoreType.DMA((ne,)),
            ],
        ),
        out_shape=jax.ShapeDtypeStruct((bsz, seq, d), x.dtype),
        compiler_params=pltpu.CompilerParams(dimension_semantics=("arbitrary",) * 2,
                                             vmem_limit_bytes=VMEM_LIMIT),
        name="combine",
    )(start, *([hs2] * cpw), *([slot_t] * cpw), y_e, final_g[None])
    return out
```

```python
import functools

import jax
import jax.numpy as jnp
from jax import lax
from jax.experimental import pallas as pl
from jax.experimental.pallas import tpu as pltpu

LRU_BLOCKS = 8
LRU_C = 8.0
CONV_WIDTH = 4
CONV_LEFT = 2
RET_HEADS = 4
CHUNK = 128
ROPE_BASE = 10000.0
EC_CAPACITY = 2
EPS = 1e-6

LANES = 128
SUBLANES = 8
BF16_ROWS = 16
MXU_DIM = 256
V7X_VMEM_BYTES = 64 * 1024 * 1024
VMEM_LIMIT = V7X_VMEM_BYTES * 13 // 16

DISPATCH_WINDOW = 3 * BF16_ROWS
DISPATCH_WINDOW_PAIR = 4 * BF16_ROWS
COMBINE_WINDOW = 4 * BF16_ROWS
COMBINE_CHUNKS = 2

F32 = jnp.float32
BF16 = jnp.bfloat16


def _round_up(x, m):
    return (x + m - 1) // m * m


def _largest_divisor_leq(n, k):
    for d in range(min(n, k), 0, -1):
        if n % d == 0:
            return d
    return 1


def _row_chunk(head_ref, x_refs, k):
    piece = x_refs[k][...]
    if k == 0:
        piece = jnp.where(pl.program_id(1) == 0, head_ref[...], piece)
    return piece


def _inproj_kernel(head_ref, *refs, tc, cb, hd, q_scale):
    x_refs = refs[0:tc]
    g_ref, w_ref, cos_ref, sin_ref, z_ref = refs[tc:]
    hs = []
    for k in range(tc):
        x = _row_chunk(head_ref, x_refs, k)
        ms = jnp.mean(x * x, axis=-1, keepdims=True)
        hs.append((x * lax.rsqrt(ms + EPS) * g_ref[...]).astype(BF16))
    h = jnp.concatenate(hs, axis=0)
    cos = cos_ref[...]
    sin = sin_ref[...]
    for j in (1, 5, 2, 3, 0, 4):
        zj = jnp.dot(h, w_ref[:, j * cb:(j + 1) * cb], preferred_element_type=F32)
        if j in (2, 3):
            for hh in range(cb // hd):
                t = zj[:, hh * hd:(hh + 1) * hd]
                r = pltpu.roll(t, hd // 2, axis=1)
                t = t * cos + r * sin
                if j == 2:
                    t = t * q_scale
                z_ref[:, j * cb + hh * hd:j * cb + (hh + 1) * hd] = t.astype(z_ref.dtype)
        else:
            if j == 1:
                zj = jax.nn.gelu(zj, approximate=True)
            elif j == 5:
                zj = jax.nn.silu(zj)
            z_ref[:, j * cb:(j + 1) * cb] = zj.astype(z_ref.dtype)


def _interleave_in(src_ref, r0, slab_ref, dst_ref, seg, off):
    nsl, t_t = slab_ref.shape[0], slab_ref.shape[1]
    for c in range(nsl):
        slab_ref[c] = src_ref[pl.ds(r0, t_t), c * LANES:(c + 1) * LANES].astype(F32)
    for j in range(seg):
        dst_ref[off + j] = jnp.concatenate(
            [slab_ref[c, pl.ds(j, SUBLANES, stride=seg), :] for c in range(nsl)], axis=1)


def _interleave_out(src_ref, slab_ref, dst_ref, r0, seg):
    nsl, t_t = slab_ref.shape[0], slab_ref.shape[1]
    for j in range(seg):
        row = src_ref[j]
        for c in range(nsl):
            slab_ref[c, pl.ds(j, SUBLANES, stride=seg), :] = row[:, c * LANES:(c + 1) * LANES]
    for c in range(nsl):
        dst_ref[pl.ds(r0, t_t), c * LANES:(c + 1) * LANES] = slab_ref[c].astype(dst_ref.dtype)


def _scan_tile(a_ref, b_ref, h_ref, acc_ref, carry_ref, seg, forward):
    w = a_ref.shape[2]

    unroll = 4 if seg % 4 == 0 else 1

    def body(jb, carry):
        h, acc = carry
        for jo in range(unroll):
            jj = jb * unroll + jo
            j = jj if forward else seg - 1 - jj
            a = a_ref[j]
            h = a * h + b_ref[j]
            acc = a * acc
            h_ref[j] = h
            acc_ref[j] = acc
        return h, acc

    h_end, a_end = lax.fori_loop(0, seg // unroll, body,
                                 (jnp.zeros((SUBLANES, w), F32), jnp.ones((SUBLANES, w), F32)))
    c = carry_ref[...]
    rows = [None] * SUBLANES
    order = range(SUBLANES) if forward else range(SUBLANES - 1, -1, -1)
    for s in order:
        rows[s] = c
        c = a_end[s:s + 1, :] * c + h_end[s:s + 1, :]
    carry_ref[...] = c
    cvec = jnp.concatenate(rows, axis=0)
    return h_ref[...] + acc_ref[...] * cvec[None]


def _lru_kernel(xm_ref, xp_ref, xn_ref, gate_ref, cw_ref, cb_ref, wr_ref, wi_ref, br_ref, bi_ref,
                lam_ref, og_ref, y_ref, hb_ref, slab_ref, xext_ref, a_ref, b_ref, h_ref, acc_ref, carry_ref,
                *, n_t, t_t, pad):
    p = pl.program_id(1)

    @pl.when(pl.program_id(2) == 0)
    def _():
        carry_ref[...] = jnp.zeros_like(carry_ref)

    nsub = xm_ref.shape[0] // t_t

    def sub_tile(k, carry):
        kk = jnp.where(p == 0, nsub - 1 - k, k)
        blk = jnp.where(p == 0, pl.num_programs(2) - 1 - pl.program_id(2), pl.program_id(2))
        _lru_tile(kk, blk * nsub + kk, xm_ref, xp_ref, xn_ref, gate_ref, cw_ref, cb_ref, wr_ref, wi_ref,
                  br_ref, bi_ref, lam_ref, og_ref, y_ref, hb_ref, slab_ref, xext_ref, a_ref, b_ref, h_ref,
                  acc_ref, carry_ref, n_t=n_t, t_t=t_t, pad=pad)
        return carry

    lax.fori_loop(0, nsub, sub_tile, 0)


def _lru_tile(kk, ti, xm_ref, xp_ref, xn_ref, gate_ref, cw_ref, cb_ref, wr_ref, wi_ref, br_ref, bi_ref,
              lam_ref, og_ref, y_ref, hb_ref, slab_ref, xext_ref, a_ref, b_ref, h_ref, acc_ref, carry_ref,
              *, n_t, t_t, pad):
    p = pl.program_id(1)
    w = xm_ref.shape[1]
    seg = t_t // SUBLANES
    nsub = xm_ref.shape[0] // t_t
    nhalo = xp_ref.shape[0]
    r0 = pl.multiple_of(kk * t_t, BF16_ROWS)

    _interleave_in(xm_ref, r0, slab_ref, xext_ref, seg, CONV_LEFT)
    sub = lax.broadcasted_iota(jnp.int32, (SUBLANES, w), 0)
    before = xm_ref[pl.ds(pl.multiple_of(jnp.maximum(r0 - nhalo, 0), BF16_ROWS), nhalo), :]
    after = xm_ref[pl.ds(pl.multiple_of(jnp.minimum(r0 + t_t, (nsub - 1) * t_t), BF16_ROWS), nhalo), :]
    xp8 = jnp.where(kk == 0, xp_ref[...], before).astype(F32)
    xn8 = jnp.where(kk == nsub - 1, xn_ref[...], after).astype(F32)
    xp8 = jnp.where(ti == 0, 0.0, xp8)
    xn8 = jnp.where(ti == n_t - 1, 0.0, xn8)
    for back in range(1, CONV_LEFT + 1):
        xext_ref[CONV_LEFT - back] = jnp.where(
            sub == 0, xp8[nhalo - back:nhalo - back + 1, :],
            pltpu.roll(xext_ref[CONV_LEFT + seg - back], 1, axis=0))
    for fwd in range(CONV_WIDTH - 1 - CONV_LEFT):
        xext_ref[CONV_LEFT + seg + fwd] = jnp.where(
            sub == SUBLANES - 1, xn8[fwd:fwd + 1, :],
            pltpu.roll(xext_ref[CONV_LEFT + fwd], SUBLANES - 1, axis=0))
    u3 = jnp.zeros((seg, SUBLANES, w), F32) + cb_ref[...]
    for k in range(CONV_WIDTH):
        u3 = u3 + cw_ref[k:k + 1, :] * xext_ref[k:k + seg]
    u = u3.reshape(t_t, w)

    ub = u.astype(BF16)
    nh = w // MXU_DIM
    pre_r = jnp.concatenate(
        [jnp.dot(ub[:, q * MXU_DIM:(q + 1) * MXU_DIM], wr_ref[q], preferred_element_type=F32)
         for q in range(nh)], axis=1) + br_ref[...]
    pre_i = jnp.concatenate(
        [jnp.dot(ub[:, q * MXU_DIM:(q + 1) * MXU_DIM], wi_ref[q], preferred_element_type=F32)
         for q in range(nh)], axis=1) + bi_ref[...]
    tr = jnp.tanh(pre_r)
    i = 0.5 + 0.5 * jnp.tanh(pre_i)
    nl = -lam_ref[...]
    en = jnp.exp(-jnp.abs(nl))
    w1 = 1.0 + en
    log1p_en = jnp.where(w1 == 1.0, en, en * jnp.log(w1) / jnp.where(w1 == 1.0, 1.0, w1 - 1.0))
    half_c = (-0.5 * LRU_C) * (jnp.maximum(nl, 0.0) + log1p_en)
    log_a = half_c + half_c * tr
    a = jnp.exp(log_a)
    v = -jnp.tanh(log_a) * (1.0 + a * a)
    gx = jnp.where(v > 0.0, v * lax.rsqrt(v), 0.0) * (i * u)
    a_ref[...] = a.reshape(seg, SUBLANES, w)
    b_ref[...] = gx.reshape(seg, SUBLANES, w)

    @pl.when(jnp.logical_and(p == 1, ti * t_t < pad))
    def _():
        j3 = lax.broadcasted_iota(jnp.int32, (seg, SUBLANES, 1), 0)
        s3 = lax.broadcasted_iota(jnp.int32, (seg, SUBLANES, 1), 1)
        rowidx = ti * t_t + s3 * seg + j3
        b_ref[...] = jnp.where(rowidx >= pad, b_ref[...], 0.0)

    @pl.when(p == 0)
    def _():
        hb_ref[ti] = _scan_tile(a_ref, b_ref, h_ref, acc_ref, carry_ref, seg, False)

    @pl.when(p == 1)
    def _():
        h = _scan_tile(a_ref, b_ref, h_ref, acc_ref, carry_ref, seg, True) + hb_ref[ti]
        _interleave_in(gate_ref, r0, slab_ref, b_ref, seg, 0)
        y = h * b_ref[...]
        ms = jnp.mean(y * y, axis=-1, keepdims=True)
        h_ref[...] = y * lax.rsqrt(ms + EPS) * og_ref[...]
        _interleave_out(h_ref, slab_ref, y_ref, r0, seg)


def _retention_kernel(q_ref, k_ref, v_ref, g_ref, dmat_ref, dvec_ref, cd_ref, og_ref, y_ref,
                      sb_ref, s_ref, *, n_t, tc, hd):
    p = pl.program_id(1)
    j = pl.program_id(2)
    ti = jnp.where(p == 0, n_t - 1 - j, j)
    nheads = q_ref.shape[1] // hd

    @pl.when(j == 0)
    def _():
        s_ref[...] = jnp.zeros_like(s_ref)

    def tn_dot(x, y):
        return lax.dot_general(x, y, (((0,), (0,)), ((), ())), preferred_element_type=F32)

    def nt_dot(x, y):
        return lax.dot_general(x, y, (((1,), (1,)), ((), ())), preferred_element_type=F32)

    @pl.when(p == 0)
    def _():
        for c in reversed(range(tc)):
            cg = ti * tc + c
            for h in range(nheads):
                rs = slice(c * CHUNK, (c + 1) * CHUNK)
                cs = slice(h * hd, (h + 1) * hd)
                kk = k_ref[rs, cs].astype(F32)
                vv = v_ref[rs, cs].astype(BF16)
                sb_ref[cg, h] = s_ref[h].astype(BF16)
                s_ref[h] = s_ref[h] * cd_ref[h] + tn_dot((kk * dvec_ref[3, h]).astype(BF16), vv)

    @pl.when(p == 1)
    def _():
        for c in range(tc):
            cg = ti * tc + c
            for h in range(nheads):
                rs = slice(c * CHUNK, (c + 1) * CHUNK)
                cs = slice(h * hd, (h + 1) * hd)
                qq = q_ref[rs, cs].astype(F32)
                kk = k_ref[rs, cs].astype(F32)
                vv = v_ref[rs, cs].astype(BF16)
                sc = nt_dot(qq.astype(BF16), kk.astype(BF16)) * dmat_ref[h]
                o = jnp.dot(sc.astype(BF16), vv, preferred_element_type=F32)
                o = o + jnp.dot((qq * dvec_ref[0, h]).astype(BF16), s_ref[h].astype(BF16),
                                preferred_element_type=F32)
                o = o + jnp.dot((qq * dvec_ref[1, h]).astype(BF16), sb_ref[cg, h],
                                preferred_element_type=F32)
                s_ref[h] = s_ref[h] * cd_ref[h] + tn_dot((kk * dvec_ref[2, h]).astype(BF16), vv)
                ms = jnp.mean(o * o, axis=-1, keepdims=True)
                o = o * lax.rsqrt(ms + EPS) * og_ref[:, cs]
                y_ref[rs, cs] = (o * g_ref[rs, cs].astype(F32)).astype(y_ref.dtype)


def _outproj_kernel(yl_ref, yr_ref, head_ref, *refs, tc, ne):
    x_refs = refs[0:tc]
    w_ref, g_ref, wr2_ref, hs2_ref, h2e_ref, lg_ref = refs[tc:]
    d = hs2_ref.shape[1]
    kcut = (tc + 1) // 2
    for ks in ([range(0, kcut), range(kcut, tc)] if kcut < tc else [range(0, tc)]):
        rs = slice(ks[0] * CHUNK, (ks[-1] + 1) * CHUNK)
        xt = jnp.concatenate([_row_chunk(head_ref, x_refs, k) for k in ks], axis=0)
        ymix = jnp.concatenate([yl_ref[rs, :].astype(BF16), yr_ref[rs, :].astype(BF16)], axis=1)
        mix = jnp.dot(ymix, w_ref[...], preferred_element_type=F32)
        hs2 = xt + mix
        hs2_ref[rs, :] = hs2.astype(hs2_ref.dtype)
        ms = jnp.mean(hs2 * hs2, axis=-1, keepdims=True)
        h2 = hs2 * lax.rsqrt(ms + EPS) * g_ref[...]
        hh = h2.astype(BF16)
        h2e_ref[rs, :] = hh
        hm = (h2 - hh.astype(F32)).astype(BF16)
        w2 = wr2_ref[...]
        d_hi = jnp.dot(hh, w2, preferred_element_type=F32)
        d_mid = jnp.dot(hm, w2, preferred_element_type=F32)
        lr = d_hi[:, 0:LANES] + (d_hi[:, LANES:] + d_mid[:, 0:LANES])
        lg_ref[:, rs] = lr.T[0:ne, :]


def _route_kernel(lg_ref, slot_ref, aff_ref, off_ref, *, pad, tp, cap):
    ne, nc, _ = lg_ref.shape
    lg = lg_ref[...]
    m = jnp.max(lg, axis=0, keepdims=True)
    ex = jnp.exp(lg - m)
    aff = ex / jnp.sum(ex, axis=0, keepdims=True)
    aff_ref[...] = aff
    tpos = (lax.broadcasted_iota(jnp.int32, (nc, LANES), 0) * LANES
            + lax.broadcasted_iota(jnp.int32, (nc, LANES), 1))
    valid = jnp.logical_and(tpos >= pad, tpos < tp)[None]
    affm = jnp.where(valid, aff, -1.0)

    def count(mask):
        c = jnp.sum(jnp.where(mask, 1.0, 0.0), axis=1, keepdims=True)
        return jnp.sum(c, axis=2, keepdims=True)

    def as_float(bits):
        return lax.bitcast_convert_type(bits, F32)

    def bs_body(_, lohi):
        lo, hi = lohi
        mid = lo + lax.shift_right_logical(hi - lo, 1)
        ok = count(affm >= as_float(mid)) >= float(cap)
        return jnp.where(ok, mid, lo), jnp.where(ok, hi, mid)

    lo0 = jnp.zeros((ne, 1, 1), jnp.int32)
    hi0 = jnp.full((ne, 1, 1), 0x3F800001, jnp.int32)
    thr_bits, _ = lax.fori_loop(0, 31, bs_body, (lo0, hi0))
    thr = as_float(thr_bits)

    gt = affm > thr
    eq = affm == thr
    need = float(cap) - count(gt)

    ci = lax.broadcasted_iota(jnp.int32, (LANES, LANES), 0)
    cj = lax.broadcasted_iota(jnp.int32, (LANES, LANES), 1)
    tri = (ci <= cj).astype(BF16)
    ri = lax.broadcasted_iota(jnp.int32, (nc, nc), 0)
    rj = lax.broadcasted_iota(jnp.int32, (nc, nc), 1)
    ltri = (rj < ri).astype(BF16)

    def prefix(mask2d):
        within = jnp.dot(mask2d.astype(BF16), tri, preferred_element_type=F32)
        tot = jnp.broadcast_to(within[:, LANES - 1:LANES], (nc, LANES)).astype(BF16)
        off = jnp.dot(ltri, tot, preferred_element_type=F32)
        return within + off, off

    for e in range(ne):
        tie_incl, _ = prefix(eq[e])
        tie_excl = tie_incl - eq[e].astype(F32)
        sel = jnp.logical_or(gt[e], jnp.logical_and(eq[e], tie_excl < need[e]))
        cum, off = prefix(sel)
        slot_ref[e] = jnp.where(sel, cum.astype(jnp.int32) - 1, -1)
        off_ref[e] = off.astype(jnp.int32)


def _dispatch_kernel(start_ref, h2e_ref, slot_ref, xs_ref, *, tcd, rw, s_rows, nc1, ne):
    b = pl.program_id(0)
    g = pl.program_id(1)
    i = pl.program_id(2)
    ge = slot_ref.shape[0]
    dext = h2e_ref.shape[1]

    @pl.when(i == 0)
    def _():
        xs_ref[...] = jnp.zeros_like(xs_ref)

    rw1, rw2 = rw
    for c in range(0, tcd, 2):
        gc = min(2, tcd - c)
        rw = rw2 if gc == 2 else rw1
        r_iota = lax.broadcasted_iota(jnp.int32, (rw, 1), 0)
        base = (b * nc1 + i * tcd + c) * ne + g * ge
        s1 = [start_ref[base + gc * ne + e] for e in range(ge)]
        lo = [(start_ref[base + e] // BF16_ROWS) * BF16_ROWS for e in range(ge)]
        npass = functools.reduce(jnp.maximum, [(s1[e] - lo[e] + rw - 1) // rw for e in range(ge)])
        rows = slice(c * CHUNK, (c + gc) * CHUNK)

        def one_pass(q, lo=lo, rows=rows, rw=rw, r_iota=r_iota):
            ws = [pl.multiple_of(jnp.minimum(lo[e] + q * rw, s_rows - rw), BF16_ROWS) for e in range(ge)]
            hits = []
            for e in range(ge):
                slot_row = slot_ref[e:e + 1, rows]
                hit = (slot_row - ws[e]) == r_iota
                if not isinstance(q, int) or q != 0:
                    hit = jnp.logical_and(hit, slot_row >= lo[e] + q * rw)
                hits.append(hit.astype(BF16))
            pc = jnp.concatenate(hits, axis=0)
            for n in range(0, dext, MXU_DIM):
                cs = slice(n, min(n + MXU_DIM, dext))
                res = jnp.dot(pc, h2e_ref[rows, cs], preferred_element_type=F32)
                for e in range(ge):
                    xs_ref[e, pl.ds(ws[e], rw), cs] += res[e * rw:(e + 1) * rw].astype(BF16)

        one_pass(0)

        def extra_pass(q, carry, one_pass=one_pass):
            one_pass(q)
            return carry

        lax.fori_loop(1, npass, extra_pass, 0)


def _ffn_kernel(xs_ref, wg_ref, wu_ref, wd_ref, y_ref, wgb_ref, wub_ref, wdb_ref, yacc_ref, *, ff_blk):
    @pl.when(pl.program_id(1) == 0)
    def _():
        wgb_ref[...] = wg_ref[...].astype(BF16)
        wub_ref[...] = wu_ref[...].astype(BF16)
        wdb_ref[...] = wd_ref[...].astype(BF16)

    wg_ref, wu_ref, wd_ref = wgb_ref, wub_ref, wdb_ref
    xs = xs_ref[...]
    ff = wg_ref.shape[1]
    for f in range(ff // ff_blk):
        fs = slice(f * ff_blk, (f + 1) * ff_blk)
        a = jnp.dot(xs, wg_ref[:, fs], preferred_element_type=F32)
        u = jnp.dot(xs, wu_ref[:, fs], preferred_element_type=F32)
        act = (jax.nn.silu(a) * u).astype(BF16)
        contrib = jnp.dot(act, wd_ref[fs, :], preferred_element_type=F32)
        if f == 0:
            yacc_ref[...] = contrib
        else:
            yacc_ref[...] += contrib
    y_ref[...] = yacc_ref[...].astype(BF16)


def _combine_kernel(start_ref, *refs, c0, cpw, rc, s_rows, nc1):
    hs2_refs = refs[0:cpw]
    slot_refs = refs[cpw:2 * cpw]
    aff_refs = refs[2 * cpw:3 * cpw]
    y_hbm, fg_ref, out_ref, ybuf_ref, yextra_ref, sem, sem_x = refs[3 * cpw:]
    b = pl.program_id(0)
    cc = pl.program_id(1)
    n_c = pl.num_programs(1)
    ne = slot_refs[0].shape[0]
    d = ybuf_ref.shape[3]
    n = b * n_c + cc
    par = n % 2
    r_iota = lax.broadcasted_iota(jnp.int32, (rc, 1), 0)

    def window_lo(bb, chunk):
        base = (bb * nc1 + chunk) * ne
        return [(start_ref[base + e] // BF16_ROWS) * BF16_ROWS for e in range(ne)]

    def window_start(lo_e, q):
        return pl.multiple_of(jnp.minimum(lo_e + q * rc, s_rows - rc), BF16_ROWS)

    def first_copies(bb, chunk, buf):
        lo_ = window_lo(bb, chunk)
        return [pltpu.make_async_copy(y_hbm.at[bb, e, pl.ds(window_start(lo_[e], 0), rc)],
                                      ybuf_ref.at[buf, e], sem.at[buf, e]) for e in range(ne)]

    @pl.when(n == 0)
    def _():
        for cp in first_copies(b, cc * cpw + c0, par):
            cp.start()

    @pl.when(n + 1 < pl.num_programs(0) * n_c)
    def _():
        n1 = n + 1
        for cp in first_copies(n1 // n_c, (n1 % n_c) * cpw + c0, 1 - par):
            cp.start()

    c = cc * cpw + c0
    base = (b * nc1 + c) * ne
    s1 = [start_ref[base + cpw * ne + e] for e in range(ne)]
    lo = window_lo(b, c)
    npass = functools.reduce(jnp.maximum, [(s1[e] - lo[e] + rc - 1) // rc for e in range(ne)])

    def onehots(q):
        hits = []
        for e in range(ne):
            slot_row = jnp.concatenate([slot_refs[k][e:e + 1, :] for k in range(cpw)], axis=1)
            hit = (slot_row - window_start(lo[e], q)) == r_iota
            if not isinstance(q, int) or q != 0:
                hit = jnp.logical_and(hit, slot_row >= lo[e] + q * rc)
            gate_row = jnp.concatenate([aff_refs[k][e:e + 1, :] for k in range(cpw)], axis=1)
            hits.append(jnp.where(hit, gate_row, 0.0).astype(BF16))
        return jnp.concatenate(hits, axis=0)

    def tn_dot(pc, yy):
        return lax.dot_general(pc, yy, (((0,), (0,)), ((), ())), preferred_element_type=F32)

    pc0 = onehots(0)
    for cp in first_copies(b, c, par):
        cp.wait()
    acc0 = tn_dot(pc0, ybuf_ref[par].reshape(ne * rc, d))

    def extra_pass(q, acc):
        cps = [pltpu.make_async_copy(y_hbm.at[b, e, pl.ds(window_start(lo[e], q), rc)],
                                     yextra_ref.at[e], sem_x.at[e]) for e in range(ne)]
        for cp in cps:
            cp.start()
        pcq = onehots(q)
        for cp in cps:
            cp.wait()
        return acc + tn_dot(pcq, yextra_ref[...].reshape(ne * rc, d))

    acc = lax.fori_loop(1, npass, extra_pass, acc0)
    for k in range(cpw):
        hs3 = hs2_refs[k][...].astype(F32) + acc[k * CHUNK:(k + 1) * CHUNK, :]
        ms = jnp.mean(hs3 * hs3, axis=-1, keepdims=True)
        out_ref[k * CHUNK:(k + 1) * CHUNK, :] = hs3 * lax.rsqrt(ms + EPS) * fg_ref[...]


def kernel(x, meta_tokens, norm1_g, w_in, conv_w, conv_b, lru_w_r, lru_b_r, lru_w_i, lru_b_i,
           lru_lambda, lru_out_g, ret_out_g, w_out, norm2_g, w_router, w_gate, w_up, w_down, final_g):
    bsz, seq, d = x.shape
    depth = norm1_g.shape[0]
    assert depth == 1
    n_meta = meta_tokens.shape[0]
    t_len = seq + n_meta
    pad = (-t_len) % CHUNK
    tp = t_len + pad
    nc = tp // CHUNK
    assert (pad + n_meta) % CHUNK == 0
    c0 = (pad + n_meta) // CHUNK
    rows = bsz * tp
    w_lru = conv_w.shape[-1]
    w_ret = ret_out_g.shape[-1]
    in_cols = w_in.shape[-1]
    cb = w_lru
    assert w_ret == cb and in_cols == 6 * cb and cb % MXU_DIM == 0
    hd = w_ret // RET_HEADS
    ne = w_router.shape[-1]
    ff = w_gate.shape[-1]
    cap = EC_CAPACITY * t_len // ne

    tc = _largest_divisor_leq(nc, 5)
    tm = tc * CHUNK
    n_t = nc // tc
    n_tiles = rows // tm

    assert c0 == 1 and seq % CHUNK == 0
    head = jnp.concatenate([jnp.zeros((pad, d), F32), meta_tokens.astype(F32)], axis=0)
    x4 = x.astype(F32).reshape(bsz, seq // CHUNK, CHUNK, d)

    def x_piece_spec(k):
        return pl.BlockSpec((None, None, CHUNK, d),
                            lambda b, j: (b, jnp.maximum(j * tc + k - 1, 0), 0, 0))

    row_tile_specs = [pl.BlockSpec((CHUNK, d), lambda b, j: (0, 0))] + [x_piece_spec(k) for k in range(tc)]

    half = hd // 2
    freqs = ROPE_BASE ** (-jnp.arange(half, dtype=F32) / half)
    ang = (jnp.arange(tp) - pad).astype(F32)[:, None] * freqs[None, :]
    cos2 = jnp.concatenate([jnp.cos(ang), jnp.cos(ang)], axis=1)
    sin2 = jnp.concatenate([-jnp.sin(ang), jnp.sin(ang)], axis=1)

    w_in_b = w_in[0].astype(BF16)
    w_out_b = w_out[0].astype(BF16)

    z = pl.pallas_call(
        functools.partial(_inproj_kernel, tc=tc, cb=cb, hd=hd, q_scale=float(hd) ** -0.5),
        grid=(bsz, n_t),
        in_specs=row_tile_specs + [
            pl.BlockSpec((1, d), lambda b, j: (0, 0)),
            pl.BlockSpec((d, in_cols), lambda b, j: (0, 0)),
            pl.BlockSpec((tm, hd), lambda b, j: (j, 0)),
            pl.BlockSpec((tm, hd), lambda b, j: (j, 0)),
        ],
        out_specs=pl.BlockSpec((tm, in_cols), lambda b, j: (b * n_t + j, 0)),
        out_shape=jax.ShapeDtypeStruct((rows, in_cols), BF16),
        compiler_params=pltpu.CompilerParams(dimension_semantics=("arbitrary",) * 2,
                                             vmem_limit_bytes=VMEM_LIMIT),
        name="inproj",
    )(head, *([x4] * tc), norm1_g[0][None], w_in_b, cos2, sin2)

    lb = w_lru // LRU_BLOCKS
    per_tile = MXU_DIM // lb
    eye = jnp.eye(per_tile, dtype=F32)

    def blockdiag(wb):
        wb = wb.reshape(2, w_lru // MXU_DIM, per_tile, lb, lb)
        return jnp.einsum('dhpij,pq->dhpiqj', wb, eye).reshape(2, w_lru // MXU_DIM, MXU_DIM, MXU_DIM).astype(BF16)

    wr_bd = blockdiag(0.5 * lru_w_r[0])
    wi_bd = blockdiag(0.5 * lru_w_i[0])
    t_l = max(t for t in range(32, min(tp, 512) + 1, 32) if tp % t == 0 and (t // 32) % 2 == 1)
    n_l = tp // t_l
    seg = t_l // SUBLANES
    assert seg >= CONV_WIDTH and cb % LANES == 0
    nsub = _largest_divisor_leq(n_l, 5)
    t_b = nsub * t_l
    n_lb = n_l // nsub
    hb8 = t_b // BF16_ROWS
    last8 = rows // BF16_ROWS - 1

    def ltile_of(b, p, j):
        return b * n_lb + jnp.where(p == 0, n_lb - 1 - j, j)

    y_lru = pl.pallas_call(
        functools.partial(_lru_kernel, n_t=n_l, t_t=t_l, pad=pad),
        grid=(bsz, 2, n_lb),
        in_specs=[
            pl.BlockSpec((t_b, cb), lambda b, p, j: (ltile_of(b, p, j), 0)),
            pl.BlockSpec((BF16_ROWS, cb), lambda b, p, j: (jnp.maximum(ltile_of(b, p, j) * hb8 - 1, 0), 0)),
            pl.BlockSpec((BF16_ROWS, cb), lambda b, p, j: (jnp.minimum((ltile_of(b, p, j) + 1) * hb8, last8), 0)),
            pl.BlockSpec((t_b, cb), lambda b, p, j: (b * n_lb + p * j, 1)),
            pl.BlockSpec((CONV_WIDTH, cb), lambda b, p, j: (0, 0)),
            pl.BlockSpec((1, cb), lambda b, p, j: (0, 0)),
            pl.BlockSpec((None, w_lru // MXU_DIM, MXU_DIM, MXU_DIM), lambda b, p, j: (1 - p, 0, 0, 0)),
            pl.BlockSpec((None, w_lru // MXU_DIM, MXU_DIM, MXU_DIM), lambda b, p, j: (1 - p, 0, 0, 0)),
            pl.BlockSpec((None, 1, cb), lambda b, p, j: (1 - p, 0, 0)),
            pl.BlockSpec((None, 1, cb), lambda b, p, j: (1 - p, 0, 0)),
            pl.BlockSpec((None, 1, cb), lambda b, p, j: (1 - p, 0, 0)),
            pl.BlockSpec((1, cb), lambda b, p, j: (0, 0)),
        ],
        out_specs=pl.BlockSpec((t_b, cb), lambda b, p, j: (b * n_lb + p * j, 0)),
        out_shape=jax.ShapeDtypeStruct((rows, cb), BF16),
        scratch_shapes=[
            pltpu.VMEM((n_l, seg, SUBLANES, cb), F32),
            pltpu.VMEM((cb // LANES, t_l, LANES), F32),
            pltpu.VMEM((seg + CONV_WIDTH - 1, SUBLANES, cb), F32),
            pltpu.VMEM((seg, SUBLANES, cb), F32),
            pltpu.VMEM((seg, SUBLANES, cb), F32),
            pltpu.VMEM((seg, SUBLANES, cb), F32),
            pltpu.VMEM((seg, SUBLANES, cb), F32),
            pltpu.VMEM((1, cb), F32),
        ],
        compiler_params=pltpu.CompilerParams(dimension_semantics=("arbitrary",) * 3,
                                             vmem_limit_bytes=VMEM_LIMIT),
        name="lru",
    )(z, z, z, z, conv_w[0], conv_b[0][None], wr_bd, wi_bd, 0.5 * lru_b_r[0][:, None], 0.5 * lru_b_i[0][:, None],
      lru_lambda[0][:, None], lru_out_g[0][None])

    log_g = jnp.log(1.0 - jnp.exp2(-5.0 - jnp.arange(RET_HEADS, dtype=F32)))
    idx = jnp.arange(CHUNK, dtype=F32)
    dmat = jnp.exp(log_g[:, None, None] * jnp.abs(idx[:, None] - idx[None, :]))

    def posdec(expo):
        return jnp.broadcast_to(jnp.exp(log_g[:, None] * expo[None, :])[:, :, None], (RET_HEADS, CHUNK, hd))

    dvec = jnp.stack([posdec(idx + 1.0), posdec(CHUNK - idx), posdec(CHUNK - 1.0 - idx), posdec(idx)])
    cdm = jnp.broadcast_to(jnp.exp(log_g * CHUNK)[:, None, None], (RET_HEADS, hd, hd))

    tc_r = _largest_divisor_leq(nc, 13)
    tm_r = tc_r * CHUNK
    n_r = nc // tc_r

    def rtile_of(b, p, j):
        return b * n_r + jnp.where(p == 0, n_r - 1 - j, j)

    y_ret = pl.pallas_call(
        functools.partial(_retention_kernel, n_t=n_r, tc=tc_r, hd=hd),
        grid=(bsz, 2, n_r),
        in_specs=[
            pl.BlockSpec((tm_r, cb), lambda b, p, j: (b * n_r + p * j, 2)),
            pl.BlockSpec((tm_r, cb), lambda b, p, j: (rtile_of(b, p, j), 3)),
            pl.BlockSpec((tm_r, cb), lambda b, p, j: (rtile_of(b, p, j), 4)),
            pl.BlockSpec((tm_r, cb), lambda b, p, j: (b * n_r + p * j, 5)),
            pl.BlockSpec((RET_HEADS, CHUNK, CHUNK), lambda b, p, j: (0, 0, 0)),
            pl.BlockSpec((4, RET_HEADS, CHUNK, hd), lambda b, p, j: (0, 0, 0, 0)),
            pl.BlockSpec((RET_HEADS, hd, hd), lambda b, p, j: (0, 0, 0)),
            pl.BlockSpec((1, cb), lambda b, p, j: (0, 0)),
        ],
        out_specs=pl.BlockSpec((tm_r, cb), lambda b, p, j: (b * n_r + p * j, 0)),
        out_shape=jax.ShapeDtypeStruct((rows, cb), BF16),
        scratch_shapes=[
            pltpu.VMEM((nc, RET_HEADS, hd, hd), BF16),
            pltpu.VMEM((RET_HEADS, hd, hd), F32),
        ],
        compiler_params=pltpu.CompilerParams(dimension_semantics=("arbitrary",) * 3,
                                             vmem_limit_bytes=VMEM_LIMIT),
        name="retention",
    )(z, z, z, z, dmat, dvec, cdm, ret_out_g[0][None])

    assert ne & (ne - 1) == 0 and 3 * ne <= LANES
    dext = d
    w_router_pad = jnp.pad(w_router[0].astype(F32), ((0, 0), (0, LANES - ne)))
    wr_hi = w_router_pad.astype(BF16)
    wr_mid = (w_router_pad - wr_hi.astype(F32)).astype(BF16)
    hs2, h2e, logits_t = pl.pallas_call(
        functools.partial(_outproj_kernel, tc=tc, ne=ne),
        grid=(bsz, n_t),
        in_specs=[
            pl.BlockSpec((tm, cb), lambda b, j: (b * n_t + j, 0)),
            pl.BlockSpec((tm, cb), lambda b, j: (b * n_t + j, 0)),
        ] + row_tile_specs + [
            pl.BlockSpec((d, d), lambda b, j: (0, 0)),
            pl.BlockSpec((1, d), lambda b, j: (0, 0)),
            pl.BlockSpec((d, 2 * LANES), lambda b, j: (0, 0)),
        ],
        out_specs=[
            pl.BlockSpec((tm, d), lambda b, j: (b * n_t + j, 0)),
            pl.BlockSpec((tm, dext), lambda b, j: (b * n_t + j, 0)),
            pl.BlockSpec((None, ne, tm), lambda b, j: (b, 0, j)),
        ],
        out_shape=[
            jax.ShapeDtypeStruct((rows, d), BF16),
            jax.ShapeDtypeStruct((rows, dext), BF16),
            jax.ShapeDtypeStruct((bsz, ne, tp), F32),
        ],
        compiler_params=pltpu.CompilerParams(dimension_semantics=("arbitrary",) * 2,
                                             vmem_limit_bytes=VMEM_LIMIT),
        name="outproj",
    )(y_lru, y_ret, head, *([x4] * tc), w_out_b, norm2_g[0][None],
      jnp.concatenate([wr_hi, wr_mid], axis=1))

    ncp = _round_up(nc, BF16_ROWS)
    logits4 = jnp.pad(logits_t.reshape(bsz, ne, nc, LANES), ((0, 0), (0, 0), (0, ncp - nc), (0, 0)))
    slot4, aff4, off4 = pl.pallas_call(
        functools.partial(_route_kernel, pad=pad, tp=tp, cap=cap),
        grid=(bsz,),
        in_specs=[pl.BlockSpec((None, ne, ncp, LANES), lambda b: (b, 0, 0, 0))],
        out_specs=[pl.BlockSpec((None, ne, ncp, LANES), lambda b: (b, 0, 0, 0))] * 3,
        out_shape=[
            jax.ShapeDtypeStruct((bsz, ne, ncp, LANES), jnp.int32),
            jax.ShapeDtypeStruct((bsz, ne, ncp, LANES), F32),
            jax.ShapeDtypeStruct((bsz, ne, ncp, LANES), jnp.int32),
        ],
        compiler_params=pltpu.CompilerParams(dimension_semantics=("arbitrary",),
                                             vmem_limit_bytes=VMEM_LIMIT),
        name="route",
    )(logits4)

    start = jnp.concatenate([jnp.transpose(off4[:, :, :nc, 0], (0, 2, 1)),
                             jnp.full((bsz, 1, ne), cap, jnp.int32)], axis=1).reshape(-1)
    nc1 = nc + 1
    slot_rows = slot4[:, :, :nc].reshape(bsz, ne, tp)

    s_rows = _round_up(cap, BF16_ROWS)
    tcd = _largest_divisor_leq(nc, 13)
    w_d = tcd * CHUNK
    n_d = nc // tcd
    rw = (DISPATCH_WINDOW, DISPATCH_WINDOW_PAIR)
    ge = min(ne, SUBLANES)
    assert all(s_rows >= r and (s_rows - r) % BF16_ROWS == 0 for r in rw) and ne % ge == 0
    xs = pl.pallas_call(
        functools.partial(_dispatch_kernel, tcd=tcd, rw=rw, s_rows=s_rows, nc1=nc1, ne=ne),
        grid_spec=pltpu.PrefetchScalarGridSpec(
            num_scalar_prefetch=1,
            grid=(bsz, ne // ge, n_d),
            in_specs=[
                pl.BlockSpec((w_d, dext), lambda b, g, i, st: (b * n_d + i, 0)),
                pl.BlockSpec((None, ge, w_d), lambda b, g, i, st: (b, g, i)),
            ],
            out_specs=pl.BlockSpec((None, ge, s_rows, dext), lambda b, g, i, st: (b, g, 0, 0)),
        ),
        out_shape=jax.ShapeDtypeStruct((bsz, ne, s_rows, dext), BF16),
        compiler_params=pltpu.CompilerParams(dimension_semantics=("arbitrary",) * 3,
                                             vmem_limit_bytes=VMEM_LIMIT),
        name="dispatch",
    )(start, h2e, slot_rows)

    ff_blk = 256 if ff % 256 == 0 else ff
    y_e = pl.pallas_call(
        functools.partial(_ffn_kernel, ff_blk=ff_blk),
        grid=(ne, bsz),
        in_specs=[
            pl.BlockSpec((None, None, s_rows, dext), lambda e, b: (b, e, 0, 0)),
            pl.BlockSpec((None, d, ff), lambda e, b: (e, 0, 0)),
            pl.BlockSpec((None, d, ff), lambda e, b: (e, 0, 0)),
            pl.BlockSpec((None, ff, d), lambda e, b: (e, 0, 0)),
        ],
        out_specs=pl.BlockSpec((None, None, s_rows, d), lambda e, b: (b, e, 0, 0)),
        out_shape=jax.ShapeDtypeStruct((bsz, ne, s_rows, d), BF16),
        scratch_shapes=[
            pltpu.VMEM((d, ff), BF16),
            pltpu.VMEM((d, ff), BF16),
            pltpu.VMEM((ff, d), BF16),
            pltpu.VMEM((s_rows, d), F32),
        ],
        compiler_params=pltpu.CompilerParams(dimension_semantics=("arbitrary",) * 2,
                                             vmem_limit_bytes=VMEM_LIMIT),
        name="ffn",
    )(xs, w_gate[0], w_up[0], w_down[0])

    rc = COMBINE_WINDOW
    cpw = COMBINE_CHUNKS if (nc - c0) % COMBINE_CHUNKS == 0 else 1
    assert (s_rows - rc) % BF16_ROWS == 0
    slot_t = jnp.transpose(slot4[:, :, :nc], (0, 2, 1, 3))
    aff_t = jnp.transpose(aff4[:, :, :nc], (0, 2, 1, 3))
    out = pl.pallas_call(
        functools.partial(_combine_kernel, c0=c0, cpw=cpw, rc=rc, s_rows=s_rows, nc1=nc1),
        grid_spec=pltpu.PrefetchScalarGridSpec(
            num_scalar_prefetch=1,
            grid=(bsz, (nc - c0) // cpw),
            in_specs=(
                [pl.BlockSpec((CHUNK, d), functools.partial(
                    lambda b, c, st, k: (b * nc + c * cpw + c0 + k, 0), k=k)) for k in range(cpw)]
                + [pl.BlockSpec((None, None, ne, LANES), functools.partial(
                    lambda b, c, st, k: (b, c * cpw + c0 + k, 0, 0), k=k)) for k in range(cpw)] * 2
                + [pl.BlockSpec(memory_space=pl.ANY),
                   pl.BlockSpec((1, d), lambda b, c, st: (0, 0))]),
            out_specs=pl.BlockSpec((None, cpw * CHUNK, d), lambda b, c, st: (b, c, 0)),
            scratch_shapes=[
                pltpu.VMEM((2, ne, rc, d), BF16),
                pltpu.VMEM((ne, rc, d), BF16),
                pltpu.SemaphoreType.DMA((2, ne)),
                pltpu.SemaphoreType.DMA((ne,)),
            ],
        ),
        out_shape=jax.ShapeDtypeStruct((bsz, seq, d), x.dtype),
        compiler_params=pltpu.CompilerParams(dimension_semantics=("arbitrary",) * 2,
                                             vmem_limit_bytes=VMEM_LIMIT),
        name="combine",
    )(start, *([hs2] * cpw), *([slot_t] * cpw), *([aff_t] * cpw), y_e, final_g[None])
    return out
```
